```python
import math
import jax, jax.numpy as jnp
from jax import lax
import numpy as np

D_MODEL = 1024
BATCH = 16
SEQ = 2048
DEPTH = 1

GMLP_WIDTH = 512
GMLP_GROUPS = 4
GMLP_GROUP_DIM = GMLP_WIDTH // GMLP_GROUPS
CHUNK = 128

ATTN_PATTERNS = ((128, 1), (512, 4), (2048, 16))
N_ATTN_GROUPS = len(ATTN_PATTERNS)
HEADS_PER_GROUP = 4
HEAD_DIM = 128
ATTN_GROUP_WIDTH = HEADS_PER_GROUP * HEAD_DIM
ATTN_WIDTH = N_ATTN_GROUPS * ATTN_GROUP_WIDTH
ROPE_DIM = HEAD_DIM // 4
ROPE_THETA = 500000.0

N_BRANCHES = 2
D_FF = 4 * D_MODEL
D_IN = 2 * GMLP_WIDTH + 3 * ATTN_WIDTH + N_BRANCHES * D_MODEL
NORM_EPS = 1e-6
MASK_VALUE = -1e30

kernel_name = "hybrid_gmlp_dilated_attn_encoder_block"


def rms_norm(x, gain):
    xf = x.astype(jnp.float32)
    y = xf * lax.rsqrt(jnp.mean(xf * xf, axis=-1, keepdims=True) + NORM_EPS)
    return (y * gain.astype(jnp.float32)).astype(x.dtype)


def layer_norm(x, gain, bias):
    xf = x.astype(jnp.float32)
    mu = jnp.mean(xf, axis=-1, keepdims=True)
    var = jnp.mean(jnp.square(xf - mu), axis=-1, keepdims=True)
    y = (xf - mu) * lax.rsqrt(var + NORM_EPS)
    return (y * gain.astype(jnp.float32) + bias.astype(jnp.float32)).astype(x.dtype)


def partial_rope(t):
    S = t.shape[1]
    half = ROPE_DIM // 2
    inv_freq = ROPE_THETA ** (-jnp.arange(0, ROPE_DIM, 2, dtype=jnp.float32) / ROPE_DIM)
    ang = jnp.arange(S, dtype=jnp.float32)[:, None] * inv_freq[None, :]
    cos = jnp.cos(ang)[None, :, None, :]
    sin = jnp.sin(ang)[None, :, None, :]
    tf = t.astype(jnp.float32)
    x1, x2, rest = tf[..., :half], tf[..., half:ROPE_DIM], tf[..., ROPE_DIM:]
    out = jnp.concatenate([x1 * cos - x2 * sin, x2 * cos + x1 * sin, rest], axis=-1)
    return out.astype(t.dtype)


def dilated_window_attention(q, k, v, dilation, radius):
    B, S, H, Dh = q.shape
    L = S // dilation
    nb = -(-L // radius)
    Lp = nb * radius

    def strided(t):
        return t.astype(jnp.float32).reshape(B, L, dilation, H, Dh).transpose(0, 2, 1, 3, 4)

    qs, ks, vs = strided(q), strided(k), strided(v)
    qb = jnp.pad(qs, ((0, 0), (0, 0), (0, Lp - L), (0, 0), (0, 0))).reshape(B, dilation, nb, radius, H, Dh)

    def windows(t):
        tb = jnp.pad(t, ((0, 0), (0, 0), (radius, Lp - L + radius), (0, 0), (0, 0)))
        tb = tb.reshape(B, dilation, nb + 2, radius, H, Dh)
        return jnp.concatenate([tb[:, :, :-2], tb[:, :, 1:-1], tb[:, :, 2:]], axis=3)

    kw, vw = windows(ks), windows(vs)
    blk = jnp.arange(nb)[:, None, None]
    qpos = blk * radius + jnp.arange(radius)[None, :, None]
    kpos = blk * radius - radius + jnp.arange(3 * radius)[None, None, :]
    valid = (jnp.abs(qpos - kpos) <= radius) & (kpos >= 0) & (kpos < L)

    scale = 1.0 / math.sqrt(Dh)
    s = jnp.einsum('bdnqhe,bdnkhe->bdnhqk', qb, kw) * scale
    s = jnp.where(valid[None, None, :, None], s, MASK_VALUE)
    m = jnp.max(s, axis=-1, keepdims=True)
    p = jnp.exp(s - m)
    denom = jnp.sum(p, axis=-1, keepdims=True)
    o = jnp.einsum('bdnhqk,bdnkhe->bdnhqe', p, vw) / denom
    lse = (m + jnp.log(denom))[..., 0]

    o = o.transpose(0, 1, 2, 4, 3, 5).reshape(B, dilation, Lp, H, Dh)[:, :, :L]
    o = o.transpose(0, 2, 1, 3, 4).reshape(B, S, H, Dh)
    lse = lse.transpose(0, 1, 2, 4, 3).reshape(B, dilation, Lp, H)[:, :, :L]
    lse = lse.transpose(0, 2, 1, 3).reshape(B, S, H)
    return o, lse


def gmlp_spatial_gating(z, ln_gain, ln_bias, w_spatial, b_spatial):
    B, S, _ = z.shape
    u, v = z[..., :GMLP_WIDTH], z[..., GMLP_WIDTH:]
    v = layer_norm(v, ln_gain, ln_bias)
    v = v.reshape(B, S // CHUNK, CHUNK, GMLP_GROUPS, GMLP_GROUP_DIM)
    sv = jnp.einsum('bcsge,gts->bctge', v, w_spatial) + b_spatial.T[None, None, :, :, None]
    return u * sv.reshape(B, S, GMLP_WIDTH)


def setup_inputs(seed: int = 0) -> dict:
    key = jax.random.key(seed)
    ks = jax.random.split(key, 17)
    f32 = jnp.float32

    def nrm(k, shape, scale):
        return jax.random.normal(k, shape, f32) * scale

    def gain(k, shape):
        return 1.0 + 0.05 * jax.random.normal(k, shape, f32)

    return {
        "x": jax.random.normal(ks[0], (BATCH, SEQ, D_MODEL), f32),
        "norm_mix_pre": gain(ks[1], (DEPTH, D_MODEL)),
        "w_in": nrm(ks[2], (DEPTH, D_MODEL, D_IN), D_MODEL ** -0.5),
        "b_gate": nrm(ks[3], (DEPTH, N_BRANCHES * D_MODEL), 0.02),
        "ln_v_gain": gain(ks[4], (DEPTH, GMLP_WIDTH)),
        "ln_v_bias": nrm(ks[5], (DEPTH, GMLP_WIDTH), 0.02),
        "w_spatial": nrm(ks[6], (DEPTH, GMLP_GROUPS, CHUNK, CHUNK), CHUNK ** -0.5),
        "b_spatial": gain(ks[7], (DEPTH, GMLP_GROUPS, CHUNK)),
        "w_branch_a": nrm(ks[8], (DEPTH, GMLP_WIDTH, D_MODEL), GMLP_WIDTH ** -0.5),
        "w_branch_b": nrm(ks[9], (DEPTH, ATTN_GROUP_WIDTH, D_MODEL), ATTN_GROUP_WIDTH ** -0.5),
        "w_out": nrm(ks[10], (DEPTH, D_MODEL, D_MODEL), D_MODEL ** -0.5),
        "norm_mix_post": gain(ks[11], (DEPTH, D_MODEL)),
        "norm_mlp_pre": gain(ks[12], (DEPTH, D_MODEL)),
        "w_up": nrm(ks[13], (DEPTH, D_MODEL, D_FF), D_MODEL ** -0.5),
        "w_down": nrm(ks[14], (DEPTH, D_FF, D_MODEL), D_FF ** -0.5),
        "norm_mlp_post": gain(ks[15], (DEPTH, D_MODEL)),
    }


def reference(x, norm_mix_pre, w_in, b_gate, ln_v_gain, ln_v_bias, w_spatial, b_spatial,
              w_branch_a, w_branch_b, w_out, norm_mix_post, norm_mlp_pre, w_up, w_down,
              norm_mlp_post):
    B, S, D = x.shape
    split_points = list(np.cumsum([GMLP_WIDTH * 2, ATTN_WIDTH, ATTN_WIDTH, ATTN_WIDTH]))
    h = x
    for l in range(DEPTH):
        n = rms_norm(h, norm_mix_pre[l])
        proj = jnp.einsum('bsd,de->bse', n, w_in[l])
        z_gmlp, q, k, v, gates = jnp.split(proj, split_points, axis=-1)

        y_a = gmlp_spatial_gating(jax.nn.gelu(z_gmlp), ln_v_gain[l], ln_v_bias[l],
                                  w_spatial[l], b_spatial[l])

        n_heads = N_ATTN_GROUPS * HEADS_PER_GROUP
        q = partial_rope(q.reshape(B, S, n_heads, HEAD_DIM))
        k = partial_rope(k.reshape(B, S, n_heads, HEAD_DIM))
        v = v.reshape(B, S, n_heads, HEAD_DIM)
        outs, lses = [], []
        for g, (window, dilation) in enumerate(ATTN_PATTERNS):
            sl = slice(g * HEADS_PER_GROUP, (g + 1) * HEADS_PER_GROUP)
            o_g, lse_g = dilated_window_attention(q[:, :, sl], k[:, :, sl], v[:, :, sl],
                                                  dilation, window // (2 * dilation))
            outs.append(o_g)
            lses.append(lse_g)
        o_all = jnp.stack(outs, axis=0)
        w_mix = jax.nn.softmax(jnp.stack(lses, axis=0), axis=0)
        y_b = jnp.sum(w_mix[..., None] * o_all, axis=0).reshape(B, S, ATTN_GROUP_WIDTH).astype(h.dtype)

        g_all = jax.nn.sigmoid((gates + b_gate[l]).astype(jnp.float32)).astype(h.dtype)
        g_a, g_b = g_all[..., :D_MODEL], g_all[..., D_MODEL:]
        merged = (g_a * jnp.einsum('bse,ed->bsd', y_a, w_branch_a[l])
                  + g_b * jnp.einsum('bse,ed->bsd', y_b, w_branch_b[l]))
        mix_out = jnp.einsum('bsd,de->bse', merged, w_out[l])
        h = h + rms_norm(mix_out, norm_mix_post[l])

        n2 = rms_norm(h, norm_mlp_pre[l])
        hid = jnp.square(jax.nn.relu(jnp.einsum('bsd,df->bsf', n2, w_up[l])))
        mlp_out = jnp.einsum('bsf,fd->bsd', hid, w_down[l])
        h = h + rms_norm(mlp_out, norm_mlp_post[l])
    return h
```

```python
import functools
import math

import jax
import jax.numpy as jnp
from jax import lax
from jax.experimental import pallas as pl
from jax.experimental.pallas import tpu as pltpu

GMLP_WIDTH = 512
GMLP_GROUPS = 4
CHUNK = 128
ATTN_PATTERNS = ((128, 1), (512, 4), (2048, 16))
N_GROUPS = len(ATTN_PATTERNS)
HEADS_PER_GROUP = 4
HEAD_DIM = 128
GROUP_WIDTH = HEADS_PER_GROUP * HEAD_DIM
ROPE_DIM = HEAD_DIM // 4
ROPE_THETA = 500000.0
NORM_EPS = 1e-6
MASK_VALUE = -1e30
RADIUS = 64
assert all(w // (2 * d) == RADIUS for w, d in ATTN_PATTERNS)

LANES = 128
VMEM_LIMIT_BYTES = 56 * 1024 * 1024

TN = 512
PROLOGUE_ROWS = 256
PROJ_ROWS = 512
Q_BLOCK = 128
COMBINE_ROWS = 256
MIX_ROWS = 512
MLP_ROWS = 512
MLP_FF_CHUNK = 1024

BF16 = jnp.bfloat16
F32 = jnp.float32

TILE_U, TILE_V, TILE_Q, TILE_K, TILE_AV, TILE_GATE = 0, 1, 2, 5, 8, 11


def _gelu_tanh(x):
    c = math.sqrt(2.0 / math.pi)
    return x * (0.5 * (1.0 + jnp.tanh(c * (x + 0.044715 * (x * x * x)))))


def _proj_kernel(x_ref, gain_ref, w_ref, bg_ref, lng_ref, lnb_ref, rope_ref, o_ref,
                 lhs_ref, slab_ref, *, seq, d_model):
    j = pl.program_id(1)
    n_slabs = d_model // LANES

    @pl.when(j == 0)
    def _prologue():
        def body(c, carry):
            r0 = pl.multiple_of(c * PROLOGUE_ROWS, PROLOGUE_ROWS)
            xc = x_ref[pl.ds(r0, PROLOGUE_ROWS), :]
            ms = jnp.mean(xc * xc, axis=-1, keepdims=True)
            n = xc * lax.rsqrt(ms + NORM_EPS) * gain_ref[...]
            lhs_ref[0, pl.ds(r0, PROLOGUE_ROWS), :] = n.astype(BF16)
            for s in range(n_slabs):
                slab_ref[s] = n[:, s * LANES:(s + 1) * LANES]
            for gi, (_, d) in enumerate(ATTN_PATTERNS):
                if d == 1:
                    continue
                sub_len = seq // d
                cnt = PROLOGUE_ROWS // d
                for r in range(d):
                    dst0 = pl.multiple_of(r * sub_len + c * cnt, cnt)
                    for s in range(n_slabs):
                        lhs_ref[gi, pl.ds(dst0, cnt), s * LANES:(s + 1) * LANES] = (
                            slab_ref[s, pl.ds(r, cnt, stride=d), :].astype(BF16))
            return carry

        lax.fori_loop(0, seq // PROLOGUE_ROWS, body, 0)

    def matmul_rows(lhs_idx, epilogue):
        def body(c, carry):
            r0 = pl.multiple_of(c * PROJ_ROWS, PROJ_ROWS)
            acc = jnp.dot(lhs_ref[lhs_idx, pl.ds(r0, PROJ_ROWS), :], w_ref[...],
                          preferred_element_type=F32)
            o_ref[pl.ds(r0, PROJ_ROWS), :] = epilogue(acc, r0).astype(BF16)
            return carry

        lax.fori_loop(0, seq // PROJ_ROWS, body, 0)

    @pl.when(j == TILE_U)
    def _u():
        matmul_rows(0, lambda acc, r0: _gelu_tanh(acc))

    @pl.when(j == TILE_V)
    def _v():
        def epi(acc, r0):
            z = _gelu_tanh(acc)
            mu = jnp.mean(z, axis=-1, keepdims=True)
            zc = z - mu
            var = jnp.mean(zc * zc, axis=-1, keepdims=True)
            return zc * lax.rsqrt(var + NORM_EPS) * lng_ref[...] + lnb_ref[...]

        matmul_rows(0, epi)

    @pl.when((j >= TILE_Q) & (j < TILE_AV))
    def _qk():
        lane = lax.broadcasted_iota(jnp.int32, (PROJ_ROWS, HEAD_DIM), 1)
        first_half = lane < (ROPE_DIM // 2)

        def epi(acc, r0):
            cos = rope_ref[0, pl.ds(r0, PROJ_ROWS), :]
            sin = rope_ref[1, pl.ds(r0, PROJ_ROWS), :]
            heads = []
            for h in range(HEADS_PER_GROUP):
                t = acc[:, h * HEAD_DIM:(h + 1) * HEAD_DIM]
                partner = jnp.where(first_half,
                                    pltpu.roll(t, HEAD_DIM - ROPE_DIM // 2, 1),
                                    pltpu.roll(t, ROPE_DIM // 2, 1))
                heads.append(t * cos + partner * sin)
            return jnp.concatenate(heads, axis=1)

        matmul_rows((j - TILE_Q) % N_GROUPS, epi)

    @pl.when((j >= TILE_AV) & (j < TILE_GATE))
    def _av():
        matmul_rows(j - TILE_AV, lambda acc, r0: acc)

    @pl.when(j >= TILE_GATE)
    def _gate():
        matmul_rows(0, lambda acc, r0: 1.0 / (1.0 + jnp.exp(-(acc + bg_ref[...]))))


def _proj_call(x, gain, w_in, b_gate, ln_gain, ln_bias, rope_tab):
    bsz, seq, d_model = x.shape
    n_tiles = w_in.shape[1] // TN
    n_gate_tiles = n_tiles - TILE_GATE
    kern = functools.partial(_proj_kernel, seq=seq, d_model=d_model)
    return pl.pallas_call(
        kern,
        name="proj",
        grid=(bsz, n_tiles),
        in_specs=[
            pl.BlockSpec((None, seq, d_model), lambda b, j: (b, 0, 0)),
            pl.BlockSpec((1, d_model), lambda b, j: (0, 0)),
            pl.BlockSpec((d_model, TN), lambda b, j: (0, j)),
            pl.BlockSpec((1, TN), lambda b, j: (0, jnp.clip(j - TILE_GATE, 0, n_gate_tiles - 1))),
            pl.BlockSpec((1, TN), lambda b, j: (0, 0)),
            pl.BlockSpec((1, TN), lambda b, j: (0, 0)),
            pl.BlockSpec((None, 2, seq, HEAD_DIM),
                         lambda b, j: (jnp.clip(j - TILE_Q, 0, 2 * N_GROUPS - 1), 0, 0, 0)),
        ],
        out_specs=pl.BlockSpec((None, None, seq, TN), lambda b, j: (j, b, 0, 0)),
        out_shape=jax.ShapeDtypeStruct((n_tiles, bsz, seq, TN), BF16),
        scratch_shapes=[
            pltpu.VMEM((N_GROUPS, seq, d_model), BF16),
            pltpu.VMEM((d_model // LANES, PROLOGUE_ROWS, LANES), F32),
        ],
        compiler_params=pltpu.CompilerParams(
            dimension_semantics=("arbitrary", "arbitrary"),
            vmem_limit_bytes=VMEM_LIMIT_BYTES),
    )(x, gain, w_in, b_gate, ln_gain, ln_bias, rope_tab)


def _attn_group(q_ref, k_ref, v_ref, o_slab, l_slab, gi, d, seq):
    sub_len = seq // d
    n_blk = sub_len // Q_BLOCK
    win = min(2 * Q_BLOCK, sub_len)
    row = lax.broadcasted_iota(jnp.int32, (Q_BLOCK, win), 0)
    col = lax.broadcasted_iota(jnp.int32, (Q_BLOCK, win), 1)

    def body(u, carry):
        r = u // n_blk
        n = u % n_blk
        q_off = n * Q_BLOCK
        k_off = jnp.clip(q_off - RADIUS, 0, sub_len - win)
        q0 = pl.multiple_of(r * sub_len + q_off, Q_BLOCK)
        k0 = pl.multiple_of(r * sub_len + k_off, RADIUS)
        valid = jnp.abs(row - col + (q_off - k_off)) <= RADIUS
        for h in range(HEADS_PER_GROUP):
            hs = slice(h * HEAD_DIM, (h + 1) * HEAD_DIM)
            qh = q_ref[pl.ds(q0, Q_BLOCK), hs]
            kh = k_ref[pl.ds(k0, win), hs]
            vh = v_ref[pl.ds(k0, win), hs]
            s = lax.dot_general(qh, kh, (((1,), (1,)), ((), ())), preferred_element_type=F32)
            s = jnp.where(valid, s, MASK_VALUE)
            m = jnp.max(s, axis=-1, keepdims=True)
            p = jnp.exp(s - m)
            denom = jnp.sum(p, axis=-1, keepdims=True)
            o = jnp.dot(p.astype(BF16), vh, preferred_element_type=F32) / denom
            lse = jnp.broadcast_to(m + jnp.log(denom), (Q_BLOCK, HEAD_DIM))
            if d == 1:
                rows = pl.ds(q0, Q_BLOCK)
            else:
                rows = pl.ds(q_off * d + r, Q_BLOCK, stride=d)
            o_slab[gi * HEADS_PER_GROUP + h, rows, :] = o
            l_slab[gi * HEADS_PER_GROUP + h, rows, :] = lse
        return carry

    lax.fori_loop(0, d * n_blk, body, 0)


def _attn_kernel(q_ref, k_ref, v_ref, y_ref, o_slab, l_slab, *, seq):
    g = pl.program_id(1)
    for gi, (_, d) in enumerate(ATTN_PATTERNS):
        @pl.when(g == gi)
        def _group(gi=gi, d=d):
            _attn_group(q_ref, k_ref, v_ref, o_slab, l_slab, gi, d, seq)

    @pl.when(g == N_GROUPS - 1)
    def _combine():
        def body(c, carry):
            rows = pl.ds(pl.multiple_of(c * COMBINE_ROWS, COMBINE_ROWS), COMBINE_ROWS)
            for h in range(HEADS_PER_GROUP):
                ls = [l_slab[gi * HEADS_PER_GROUP + h, rows, :] for gi in range(N_GROUPS)]
                m = functools.reduce(jnp.maximum, ls)
                es = [jnp.exp(l - m) for l in ls]
                den = functools.reduce(jnp.add, es)
                num = functools.reduce(
                    jnp.add,
                    [e * o_slab[gi * HEADS_PER_GROUP + h, rows, :] for gi, e in enumerate(es)])
                y_ref[rows, h * HEAD_DIM:(h + 1) * HEAD_DIM] = (num / den).astype(BF16)
            return carry

        lax.fori_loop(0, seq // COMBINE_ROWS, body, 0)


def _attn_call(proj):
    _, bsz, seq, _ = proj.shape
    n_slabs = N_GROUPS * HEADS_PER_GROUP

    def spec(first_tile):
        return pl.BlockSpec((None, None, seq, TN), lambda b, g: (first_tile + g, b, 0, 0))

    return pl.pallas_call(
        functools.partial(_attn_kernel, seq=seq),
        name="attn",
        grid=(bsz, N_GROUPS),
        in_specs=[spec(TILE_Q), spec(TILE_K), spec(TILE_AV)],
        out_specs=pl.BlockSpec((None, seq, GROUP_WIDTH), lambda b, g: (b, 0, 0)),
        out_shape=jax.ShapeDtypeStruct((bsz, seq, GROUP_WIDTH), BF16),
        scratch_shapes=[
            pltpu.VMEM((n_slabs, seq, LANES), F32),
            pltpu.VMEM((n_slabs, seq, LANES), F32),
        ],
        compiler_params=pltpu.CompilerParams(
            dimension_semantics=("arbitrary", "arbitrary"),
            vmem_limit_bytes=VMEM_LIMIT_BYTES),
    )(proj, proj, proj)


def _mix_kernel(zu_ref, zv_ref, ga0_ref, ga1_ref, gb0_ref, gb1_ref, yb_ref, x_ref,
                ws_ref, bsp_ref, wa_ref, wb_ref, wo_ref, gain_ref, o_ref, ya_ref):
    group_dim = GMLP_WIDTH // GMLP_GROUPS
    for c in range(MIX_ROWS // CHUNK):
        rows = slice(c * CHUNK, (c + 1) * CHUNK)
        sv = jnp.concatenate(
            [jnp.dot(ws_ref[g], zv_ref[rows, g * group_dim:(g + 1) * group_dim],
                     preferred_element_type=F32) for g in range(GMLP_GROUPS)], axis=1)
        ya_ref[rows, :] = (zu_ref[rows, :].astype(F32) * (sv + bsp_ref[...])).astype(BF16)
    a = jnp.dot(ya_ref[...], wa_ref[...], preferred_element_type=F32)
    b = jnp.dot(yb_ref[...], wb_ref[...], preferred_element_type=F32)
    gate_a = jnp.concatenate([ga0_ref[...], ga1_ref[...]], axis=1).astype(F32)
    gate_b = jnp.concatenate([gb0_ref[...], gb1_ref[...]], axis=1).astype(F32)
    merged = gate_a * a + gate_b * b
    mix = jnp.dot(merged.astype(BF16), wo_ref[...], preferred_element_type=F32)
    ms = jnp.mean(mix * mix, axis=-1, keepdims=True)
    o_ref[...] = x_ref[...] + mix * lax.rsqrt(ms + NORM_EPS) * gain_ref[...]


def _mix_call(proj, y_b, x, w_spatial, b_sp, w_a, w_b, w_out, gain):
    bsz, seq, d_model = x.shape

    def tile(t):
        return pl.BlockSpec((None, None, MIX_ROWS, TN), lambda b, i: (t, b, i, 0))

    def whole(a):
        return pl.BlockSpec(a.shape, lambda b, i: (0,) * a.ndim)

    return pl.pallas_call(
        _mix_kernel,
        name="mix",
        grid=(bsz, seq // MIX_ROWS),
        in_specs=[
            tile(TILE_U), tile(TILE_V),
            tile(TILE_GATE), tile(TILE_GATE + 1), tile(TILE_GATE + 2), tile(TILE_GATE + 3),
            pl.BlockSpec((None, MIX_ROWS, GROUP_WIDTH), lambda b, i: (b, i, 0)),
            pl.BlockSpec((None, MIX_ROWS, d_model), lambda b, i: (b, i, 0)),
            whole(w_spatial), whole(b_sp), whole(w_a), whole(w_b), whole(w_out), whole(gain),
        ],
        out_specs=pl.BlockSpec((None, MIX_ROWS, d_model), lambda b, i: (b, i, 0)),
        out_shape=jax.ShapeDtypeStruct((bsz, seq, d_model), F32),
        scratch_shapes=[pltpu.VMEM((MIX_ROWS, GMLP_WIDTH), BF16)],
        compiler_params=pltpu.CompilerParams(
            dimension_semantics=("parallel", "parallel"),
            vmem_limit_bytes=VMEM_LIMIT_BYTES),
    )(proj, proj, proj, proj, proj, proj, y_b, x, w_spatial, b_sp, w_a, w_b, w_out, gain)


def _mlp_kernel(h_ref, gpre_ref, wu_ref, wd_ref, gpost_ref, o_ref, n_ref, acc_ref, *, d_ff):
    h = h_ref[...]
    ms = jnp.mean(h * h, axis=-1, keepdims=True)
    n_ref[...] = (h * lax.rsqrt(ms + NORM_EPS) * gpre_ref[...]).astype(BF16)
    for c in range(d_ff // MLP_FF_CHUNK):
        cols = slice(c * MLP_FF_CHUNK, (c + 1) * MLP_FF_CHUNK)
        up = jnp.dot(n_ref[...], wu_ref[:, cols], preferred_element_type=F32)
        hid = jnp.square(jnp.maximum(up, 0.0)).astype(BF16)
        part = jnp.dot(hid, wd_ref[cols, :], preferred_element_type=F32)
        if c == 0:
            acc_ref[...] = part
        else:
            acc_ref[...] += part
    out = acc_ref[...]
    ms2 = jnp.mean(out * out, axis=-1, keepdims=True)
    o_ref[...] = h_ref[...] + out * lax.rsqrt(ms2 + NORM_EPS) * gpost_ref[...]


def _mlp_call(h, gain_pre, w_up, w_down, gain_post):
    bsz, seq, d_model = h.shape
    d_ff = w_up.shape[1]
    rows = bsz * seq
    h2 = h.reshape(rows, d_model)

    def whole(a):
        return pl.BlockSpec(a.shape, lambda i: (0,) * a.ndim, pipeline_mode=pl.Buffered(1))

    out = pl.pallas_call(
        functools.partial(_mlp_kernel, d_ff=d_ff),
        name="mlp",
        grid=(rows // MLP_ROWS,),
        in_specs=[
            pl.BlockSpec((MLP_ROWS, d_model), lambda i: (i, 0)),
            whole(gain_pre), whole(w_up), whole(w_down), whole(gain_post),
        ],
        out_specs=pl.BlockSpec((MLP_ROWS, d_model), lambda i: (i, 0)),
        out_shape=jax.ShapeDtypeStruct((rows, d_model), F32),
        scratch_shapes=[
            pltpu.VMEM((MLP_ROWS, d_model), BF16),
            pltpu.VMEM((MLP_ROWS, d_model), F32),
        ],
        compiler_params=pltpu.CompilerParams(
            dimension_semantics=("parallel",),
            vmem_limit_bytes=VMEM_LIMIT_BYTES),
    )(h2, gain_pre, w_up, w_down, gain_post)
    return out.reshape(bsz, seq, d_model)


def _rope_tables(seq):
    half = ROPE_DIM // 2
    inv_freq = ROPE_THETA ** (-jnp.arange(0, ROPE_DIM, 2, dtype=F32) / ROPE_DIM)
    ang = jnp.arange(seq, dtype=F32)[:, None] * inv_freq[None, :]
    cos, sin = jnp.cos(ang), jnp.sin(ang)
    ones = jnp.ones((seq, HEAD_DIM - ROPE_DIM), F32)
    cos_l = jnp.concatenate([cos, cos, ones], axis=1)
    sin_l = jnp.concatenate([-sin, sin, 0.0 * ones], axis=1)
    nat = jnp.stack([cos_l, sin_l])
    per_group = []
    for _, d in ATTN_PATTERNS:
        per_group.append(nat.reshape(2, seq // d, d, HEAD_DIM).transpose(0, 2, 1, 3).reshape(2, seq, HEAD_DIM))
    k_tab = jnp.stack(per_group)
    q_tab = k_tab * (1.0 / math.sqrt(HEAD_DIM))
    return jnp.concatenate([q_tab, k_tab], axis=0)


def _layer(h, norm_mix_pre, w_in, b_gate, ln_v_gain, ln_v_bias, w_spatial, b_spatial,
           w_branch_a, w_branch_b, w_out, norm_mix_post, norm_mlp_pre, w_up, w_down,
           norm_mlp_post):
    seq = h.shape[1]
    row = lambda v: v.reshape(1, -1).astype(F32)
    proj = _proj_call(h, row(norm_mix_pre), w_in.astype(BF16), row(b_gate), row(ln_v_gain),
                      row(ln_v_bias), _rope_tables(seq))
    y_b = _attn_call(proj)
    b_sp = jnp.repeat(b_spatial.T.astype(F32), GMLP_WIDTH // GMLP_GROUPS, axis=1)
    h = _mix_call(proj, y_b, h, w_spatial.astype(BF16), b_sp, w_branch_a.astype(BF16),
                  w_branch_b.astype(BF16), w_out.astype(BF16), row(norm_mix_post))
    return _mlp_call(h, row(norm_mlp_pre), w_up.astype(BF16), w_down.astype(BF16),
                     row(norm_mlp_post))


def kernel(x, norm_mix_pre, w_in, b_gate, ln_v_gain, ln_v_bias, w_spatial, b_spatial,
           w_branch_a, w_branch_b, w_out, norm_mix_post, norm_mlp_pre, w_up, w_down,
           norm_mlp_post):
    h = x
    for l in range(w_in.shape[0]):
        h = _layer(h, norm_mix_pre[l], w_in[l], b_gate[l], ln_v_gain[l], ln_v_bias[l],
                   w_spatial[l], b_spatial[l], w_branch_a[l], w_branch_b[l], w_out[l],
                   norm_mix_post[l], norm_mlp_pre[l], w_up[l], w_down[l], norm_mlp_post[l])
    return h
```

```python
import functools
import math

import jax
import jax.numpy as jnp
from jax import lax
from jax.experimental import pallas as pl
from jax.experimental.pallas import tpu as pltpu

GMLP_WIDTH = 512
GMLP_GROUPS = 4
CHUNK = 128
ATTN_PATTERNS = ((128, 1), (512, 4), (2048, 16))
N_GROUPS = len(ATTN_PATTERNS)
HEADS_PER_GROUP = 4
HEAD_DIM = 128
GROUP_WIDTH = HEADS_PER_GROUP * HEAD_DIM
ROPE_DIM = HEAD_DIM // 4
ROPE_THETA = 500000.0
NORM_EPS = 1e-6
MASK_VALUE = -1e30
LOG2E = math.log2(math.e)
RADIUS = 64
assert all(w // (2 * d) == RADIUS for w, d in ATTN_PATTERNS)

LANES = 128
VMEM_LIMIT_BYTES = 56 * 1024 * 1024

TN = 512
PROLOGUE_ROWS = 256
PROJ_ROWS = 512
Q_BLOCK = 128
BLOCKS_PER_STEP = 2
COMBINE_ROWS = 256
MIX_ROWS = 512
MLP_ROWS = 512
MLP_FF_CHUNK = 1024

BF16 = jnp.bfloat16
F32 = jnp.float32

TILE_U, TILE_V, TILE_Q, TILE_K, TILE_AV, TILE_GATE = 0, 1, 2, 5, 8, 11


def _gelu_tanh(x):
    c = math.sqrt(2.0 / math.pi)
    return x * (0.5 * (1.0 + jnp.tanh(c * (x + 0.044715 * (x * x * x)))))


def _proj_kernel(x_ref, gain_ref, w_ref, bg_ref, lng_ref, lnb_ref, rope_ref, o_ref,
                 lhs_ref, slab_ref, *, seq, d_model):
    j = pl.program_id(1)
    n_slabs = d_model // LANES

    @pl.when(j == 0)
    def _prologue():
        def body(c, carry):
            r0 = pl.multiple_of(c * PROLOGUE_ROWS, PROLOGUE_ROWS)
            xc = x_ref[pl.ds(r0, PROLOGUE_ROWS), :]
            ms = jnp.mean(xc * xc, axis=-1, keepdims=True)
            n = xc * lax.rsqrt(ms + NORM_EPS) * gain_ref[...]
            lhs_ref[0, pl.ds(r0, PROLOGUE_ROWS), :] = n.astype(BF16)
            for s in range(n_slabs):
                slab_ref[s] = n[:, s * LANES:(s + 1) * LANES]
            for gi, (_, d) in enumerate(ATTN_PATTERNS):
                if d == 1:
                    continue
                sub_len = seq // d
                cnt = PROLOGUE_ROWS // d
                for r in range(d):
                    dst0 = pl.multiple_of(r * sub_len + c * cnt, cnt)
                    for s in range(n_slabs):
                        lhs_ref[gi, pl.ds(dst0, cnt), s * LANES:(s + 1) * LANES] = (
                            slab_ref[s, pl.ds(r, cnt, stride=d), :].astype(BF16))
            return carry

        lax.fori_loop(0, seq // PROLOGUE_ROWS, body, 0)

    def matmul_rows(lhs_idx, epilogue):
        for c in range(seq // PROJ_ROWS):
            r0 = c * PROJ_ROWS
            acc = jnp.dot(lhs_ref[lhs_idx, pl.ds(r0, PROJ_ROWS), :], w_ref[...],
                          preferred_element_type=F32)
            o_ref[pl.ds(r0, PROJ_ROWS), :] = epilogue(acc, r0).astype(BF16)

    @pl.when(j == TILE_U)
    def _u():
        matmul_rows(0, lambda acc, r0: _gelu_tanh(acc))

    @pl.when(j == TILE_V)
    def _v():
        def epi(acc, r0):
            z = _gelu_tanh(acc)
            mu = jnp.mean(z, axis=-1, keepdims=True)
            zc = z - mu
            var = jnp.mean(zc * zc, axis=-1, keepdims=True)
            return zc * lax.rsqrt(var + NORM_EPS) * lng_ref[...] + lnb_ref[...]

        matmul_rows(0, epi)

    @pl.when((j >= TILE_Q) & (j < TILE_AV))
    def _qk():
        def epi(acc, r0):
            cos = rope_ref[0, pl.ds(r0, PROJ_ROWS), :]
            sin = rope_ref[1, pl.ds(r0, PROJ_ROWS), :]
            heads = []
            for h in range(HEADS_PER_GROUP):
                t = acc[:, h * HEAD_DIM:(h + 1) * HEAD_DIM]
                heads.append(t * cos + pltpu.roll(t, HEAD_DIM // 2, 1) * sin)
            return jnp.concatenate(heads, axis=1)

        matmul_rows((j - TILE_Q) % N_GROUPS, epi)

    @pl.when((j >= TILE_AV) & (j < TILE_GATE))
    def _av():
        matmul_rows(j - TILE_AV, lambda acc, r0: acc)

    @pl.when(j >= TILE_GATE)
    def _gate():
        matmul_rows(0, lambda acc, r0: 1.0 / (1.0 + jnp.exp(-(acc + bg_ref[...]))))


def _proj_call(x, gain, w_in, b_gate, ln_gain, ln_bias, rope_tab):
    bsz, seq, d_model = x.shape
    n_tiles = w_in.shape[1] // TN
    n_gate_tiles = n_tiles - TILE_GATE
    kern = functools.partial(_proj_kernel, seq=seq, d_model=d_model)
    return pl.pallas_call(
        kern,
        name="proj",
        grid=(bsz, n_tiles),
        in_specs=[
            pl.BlockSpec((None, seq, d_model), lambda b, j: (b, 0, 0)),
            pl.BlockSpec((1, d_model), lambda b, j: (0, 0)),
            pl.BlockSpec((d_model, TN), lambda b, j: (0, j)),
            pl.BlockSpec((1, TN), lambda b, j: (0, jnp.clip(j - TILE_GATE, 0, n_gate_tiles - 1))),
            pl.BlockSpec((1, TN), lambda b, j: (0, 0)),
            pl.BlockSpec((1, TN), lambda b, j: (0, 0)),
            pl.BlockSpec((None, 2, seq, HEAD_DIM),
                         lambda b, j: (jnp.clip(j - TILE_Q, 0, 2 * N_GROUPS - 1), 0, 0, 0)),
        ],
        out_specs=pl.BlockSpec((None, None, seq, TN), lambda b, j: (j, b, 0, 0)),
        out_shape=jax.ShapeDtypeStruct((n_tiles, bsz, seq, TN), BF16),
        scratch_shapes=[
            pltpu.VMEM((N_GROUPS, seq, d_model), BF16),
            pltpu.VMEM((d_model // LANES, PROLOGUE_ROWS, LANES), F32),
        ],
        compiler_params=pltpu.CompilerParams(
            dimension_semantics=("arbitrary", "arbitrary"),
            vmem_limit_bytes=VMEM_LIMIT_BYTES),
    )(x, gain, w_in, b_gate, ln_gain, ln_bias, rope_tab)


def _attn_group(q_ref, k_ref, v_ref, o_slab, l_slab, s_scr, p_scr, gi, d, seq):
    sub_len = seq // d
    n_blk = sub_len // Q_BLOCK
    win = min(2 * Q_BLOCK, sub_len)
    row = lax.broadcasted_iota(jnp.int32, (Q_BLOCK, win), 0)
    col = lax.broadcasted_iota(jnp.int32, (Q_BLOCK, win), 1)
    ones = jnp.ones((win, HEAD_DIM), BF16)

    def body(step, carry):
        blocks = []
        for e in range(BLOCKS_PER_STEP):
            f = step * BLOCKS_PER_STEP + e
            r = f // n_blk
            q_off = (f % n_blk) * Q_BLOCK
            k_off = jnp.clip(q_off - RADIUS, 0, sub_len - win)
            blocks.append(dict(
                r=r, q_off=q_off, shift=q_off - k_off,
                q0=pl.multiple_of(r * sub_len + q_off, Q_BLOCK),
                k0=pl.multiple_of(r * sub_len + k_off, RADIUS)))
        units = [(e, h) for e in range(BLOCKS_PER_STEP) for h in range(HEADS_PER_GROUP)]

        for u, (e, h) in enumerate(units):
            blk = blocks[e]
            hs = slice(h * HEAD_DIM, (h + 1) * HEAD_DIM)
            s_scr[u, :, :win] = lax.dot_general(
                q_ref[pl.ds(blk["q0"], Q_BLOCK), hs], k_ref[pl.ds(blk["k0"], win), hs],
                (((1,), (1,)), ((), ())), preferred_element_type=F32)

        row_max = []
        for u, (e, h) in enumerate(units):
            valid = jnp.abs(row - col + blocks[e]["shift"]) <= RADIUS
            s = jnp.where(valid, s_scr[u, :, :win], MASK_VALUE)
            m = jnp.max(s, axis=-1, keepdims=True)
            p_scr[u, :, :win] = jnp.exp2(s - m).astype(BF16)
            row_max.append(m)

        for u, (e, h) in enumerate(units):
            blk = blocks[e]
            hs = slice(h * HEAD_DIM, (h + 1) * HEAD_DIM)
            v_ext = jnp.concatenate([v_ref[pl.ds(blk["k0"], win), hs], ones], axis=1)
            oe = jnp.dot(p_scr[u, :, :win], v_ext, preferred_element_type=F32)
            denom = oe[:, HEAD_DIM:]
            o = oe[:, :HEAD_DIM] / denom
            lse = row_max[u] + jnp.log(denom) * LOG2E
            if d == 1:
                rows = pl.ds(blk["q0"], Q_BLOCK)
            else:
                rows = pl.ds(blk["q_off"] * d + blk["r"], Q_BLOCK, stride=d)
            o_slab[gi * HEADS_PER_GROUP + h, rows, :] = o
            l_slab[gi * HEADS_PER_GROUP + h, rows, :] = lse
        return carry

    lax.fori_loop(0, d * n_blk // BLOCKS_PER_STEP, body, 0)


def _attn_kernel(q_ref, k_ref, v_ref, y_ref, o_slab, l_slab, s_scr, p_scr, *, seq):
    g = pl.program_id(1)
    for gi, (_, d) in enumerate(ATTN_PATTERNS):
        @pl.when(g == gi)
        def _group(gi=gi, d=d):
            _attn_group(q_ref, k_ref, v_ref, o_slab, l_slab, s_scr, p_scr, gi, d, seq)

    @pl.when(g == N_GROUPS - 1)
    def _combine():
        def body(c, carry):
            rows = pl.ds(pl.multiple_of(c * COMBINE_ROWS, COMBINE_ROWS), COMBINE_ROWS)
            for h in range(HEADS_PER_GROUP):
                ls = [l_slab[gi * HEADS_PER_GROUP + h, rows, :] for gi in range(N_GROUPS)]
                m = functools.reduce(jnp.maximum, ls)
                es = [jnp.exp2(l - m) for l in ls]
                den = functools.reduce(jnp.add, es)
                num = functools.reduce(
                    jnp.add,
                    [e * o_slab[gi * HEADS_PER_GROUP + h, rows, :] for gi, e in enumerate(es)])
                y_ref[rows, h * HEAD_DIM:(h + 1) * HEAD_DIM] = (num / den).astype(BF16)
            return carry

        lax.fori_loop(0, seq // COMBINE_ROWS, body, 0)


def _attn_call(proj):
    _, bsz, seq, _ = proj.shape
    n_slabs = N_GROUPS * HEADS_PER_GROUP

    def spec(first_tile):
        return pl.BlockSpec((None, None, seq, TN), lambda b, g: (first_tile + g, b, 0, 0))

    return pl.pallas_call(
        functools.partial(_attn_kernel, seq=seq),
        name="attn",
        grid=(bsz, N_GROUPS),
        in_specs=[spec(TILE_Q), spec(TILE_K), spec(TILE_AV)],
        out_specs=pl.BlockSpec((None, seq, GROUP_WIDTH), lambda b, g: (b, 0, 0)),
        out_shape=jax.ShapeDtypeStruct((bsz, seq, GROUP_WIDTH), BF16),
        scratch_shapes=[
            pltpu.VMEM((n_slabs, seq, LANES), F32),
            pltpu.VMEM((n_slabs, seq, LANES), F32),
            pltpu.VMEM((BLOCKS_PER_STEP * HEADS_PER_GROUP, Q_BLOCK, 2 * Q_BLOCK), F32),
            pltpu.VMEM((BLOCKS_PER_STEP * HEADS_PER_GROUP, Q_BLOCK, 2 * Q_BLOCK), BF16),
        ],
        compiler_params=pltpu.CompilerParams(
            dimension_semantics=("arbitrary", "arbitrary"),
            vmem_limit_bytes=VMEM_LIMIT_BYTES),
    )(proj, proj, proj)


def _mix_kernel(zu_ref, zv_ref, ga0_ref, ga1_ref, gb0_ref, gb1_ref, yb_ref, x_ref,
                ws_ref, bsp_ref, wa_ref, wb_ref, wo_ref, gain_ref, o_ref, ya_ref):
    group_dim = GMLP_WIDTH // GMLP_GROUPS
    for c in range(MIX_ROWS // CHUNK):
        rows = slice(c * CHUNK, (c + 1) * CHUNK)
        sv = jnp.concatenate(
            [jnp.dot(ws_ref[g], zv_ref[rows, g * group_dim:(g + 1) * group_dim],
                     preferred_element_type=F32) for g in range(GMLP_GROUPS)], axis=1)
        ya_ref[rows, :] = (zu_ref[rows, :].astype(F32) * (sv + bsp_ref[...])).astype(BF16)
    a = jnp.dot(ya_ref[...], wa_ref[...], preferred_element_type=F32)
    b = jnp.dot(yb_ref[...], wb_ref[...], preferred_element_type=F32)
    gate_a = jnp.concatenate([ga0_ref[...], ga1_ref[...]], axis=1).astype(F32)
    gate_b = jnp.concatenate([gb0_ref[...], gb1_ref[...]], axis=1).astype(F32)
    merged = gate_a * a + gate_b * b
    mix = jnp.dot(merged.astype(BF16), wo_ref[...], preferred_element_type=F32)
    ms = jnp.mean(mix * mix, axis=-1, keepdims=True)
    o_ref[...] = x_ref[...] + mix * lax.rsqrt(ms + NORM_EPS) * gain_ref[...]


def _mix_call(proj, y_b, x, w_spatial, b_sp, w_a, w_b, w_out, gain):
    bsz, seq, d_model = x.shape

    def tile(t):
        return pl.BlockSpec((None, None, MIX_ROWS, TN), lambda b, i: (t, b, i, 0))

    def whole(a):
        return pl.BlockSpec(a.shape, lambda b, i: (0,) * a.ndim)

    return pl.pallas_call(
        _mix_kernel,
        name="mix",
        grid=(bsz, seq // MIX_ROWS),
        in_specs=[
            tile(TILE_U), tile(TILE_V),
            tile(TILE_GATE), tile(TILE_GATE + 1), tile(TILE_GATE + 2), tile(TILE_GATE + 3),
            pl.BlockSpec((None, MIX_ROWS, GROUP_WIDTH), lambda b, i: (b, i, 0)),
            pl.BlockSpec((None, MIX_ROWS, d_model), lambda b, i: (b, i, 0)),
            whole(w_spatial), whole(b_sp), whole(w_a), whole(w_b), whole(w_out), whole(gain),
        ],
        out_specs=pl.BlockSpec((None, MIX_ROWS, d_model), lambda b, i: (b, i, 0)),
        out_shape=jax.ShapeDtypeStruct((bsz, seq, d_model), F32),
        scratch_shapes=[pltpu.VMEM((MIX_ROWS, GMLP_WIDTH), BF16)],
        compiler_params=pltpu.CompilerParams(
            dimension_semantics=("parallel", "parallel"),
            vmem_limit_bytes=VMEM_LIMIT_BYTES),
    )(proj, proj, proj, proj, proj, proj, y_b, x, w_spatial, b_sp, w_a, w_b, w_out, gain)


def _mlp_kernel(h_ref, gpre_ref, wu_ref, wd_ref, gpost_ref, o_ref, n_ref, acc_ref, *, d_ff):
    h = h_ref[...]
    ms = jnp.mean(h * h, axis=-1, keepdims=True)
    n_ref[...] = (h * lax.rsqrt(ms + NORM_EPS) * gpre_ref[...]).astype(BF16)
    for c in range(d_ff // MLP_FF_CHUNK):
        cols = slice(c * MLP_FF_CHUNK, (c + 1) * MLP_FF_CHUNK)
        up = jnp.dot(n_ref[...], wu_ref[:, cols], preferred_element_type=F32)
        hid = jnp.square(jnp.maximum(up, 0.0)).astype(BF16)
        part = jnp.dot(hid, wd_ref[cols, :], preferred_element_type=F32)
        if c == 0:
            acc_ref[...] = part
        else:
            acc_ref[...] += part
    out = acc_ref[...]
    ms2 = jnp.mean(out * out, axis=-1, keepdims=True)
    o_ref[...] = h_ref[...] + out * lax.rsqrt(ms2 + NORM_EPS) * gpost_ref[...]


def _mlp_call(h, gain_pre, w_up, w_down, gain_post):
    bsz, seq, d_model = h.shape
    d_ff = w_up.shape[1]
    rows = bsz * seq
    h2 = h.reshape(rows, d_model)

    def whole(a):
        return pl.BlockSpec(a.shape, lambda i: (0,) * a.ndim, pipeline_mode=pl.Buffered(1))

    out = pl.pallas_call(
        functools.partial(_mlp_kernel, d_ff=d_ff),
        name="mlp",
        grid=(rows // MLP_ROWS,),
        in_specs=[
            pl.BlockSpec((MLP_ROWS, d_model), lambda i: (i, 0)),
            whole(gain_pre), whole(w_up), whole(w_down), whole(gain_post),
        ],
        out_specs=pl.BlockSpec((MLP_ROWS, d_model), lambda i: (i, 0)),
        out_shape=jax.ShapeDtypeStruct((rows, d_model), F32),
        scratch_shapes=[
            pltpu.VMEM((MLP_ROWS, d_model), BF16),
            pltpu.VMEM((MLP_ROWS, d_model), F32),
        ],
        compiler_params=pltpu.CompilerParams(
            dimension_semantics=("parallel",),
            vmem_limit_bytes=VMEM_LIMIT_BYTES),
    )(h2, gain_pre, w_up, w_down, gain_post)
    return out.reshape(bsz, seq, d_model)


def _rope_lane_order():
    half = ROPE_DIM // 2
    fill = HEAD_DIM // 2 - half
    return (list(range(half)) + list(range(ROPE_DIM, ROPE_DIM + fill))
            + list(range(half, ROPE_DIM)) + list(range(ROPE_DIM + fill, HEAD_DIM)))


def _permute_qk_columns(w_in):
    n_qk_heads = 2 * N_GROUPS * HEADS_PER_GROUP
    q_start = 2 * GMLP_WIDTH
    order = jnp.asarray(_rope_lane_order(), jnp.int32)
    qk_idx = (q_start + HEAD_DIM * jnp.arange(n_qk_heads, dtype=jnp.int32)[:, None]
              + order[None, :]).reshape(-1)
    idx = jnp.arange(w_in.shape[1], dtype=jnp.int32)
    idx = lax.dynamic_update_slice(idx, qk_idx, (q_start,))
    return jnp.take(w_in, idx, axis=1)


def _rope_tables(seq):
    half = ROPE_DIM // 2
    fill = HEAD_DIM // 2 - half
    inv_freq = ROPE_THETA ** (-jnp.arange(0, ROPE_DIM, 2, dtype=F32) / ROPE_DIM)
    ang = jnp.arange(seq, dtype=F32)[:, None] * inv_freq[None, :]
    cos, sin = jnp.cos(ang), jnp.sin(ang)
    ones = jnp.ones((seq, fill), F32)
    cos_l = jnp.concatenate([cos, ones, cos, ones], axis=1)
    sin_l = jnp.concatenate([-sin, 0.0 * ones, sin, 0.0 * ones], axis=1)
    nat = jnp.stack([cos_l, sin_l])
    per_group = []
    for _, d in ATTN_PATTERNS:
        per_group.append(nat.reshape(2, seq // d, d, HEAD_DIM).transpose(0, 2, 1, 3).reshape(2, seq, HEAD_DIM))
    k_tab = jnp.stack(per_group)
    q_tab = k_tab * (LOG2E / math.sqrt(HEAD_DIM))
    return jnp.concatenate([q_tab, k_tab], axis=0)


def _layer(h, norm_mix_pre, w_in, b_gate, ln_v_gain, ln_v_bias, w_spatial, b_spatial,
           w_branch_a, w_branch_b, w_out, norm_mix_post, norm_mlp_pre, w_up, w_down,
           norm_mlp_post):
    seq = h.shape[1]
    row = lambda v: v.reshape(1, -1).astype(F32)
    proj = _proj_call(h, row(norm_mix_pre), _permute_qk_columns(w_in).astype(BF16), row(b_gate), row(ln_v_gain),
                      row(ln_v_bias), _rope_tables(seq))
    y_b = _attn_call(proj)
    b_sp = jnp.repeat(b_spatial.T.astype(F32), GMLP_WIDTH // GMLP_GROUPS, axis=1)
    h = _mix_call(proj, y_b, h, w_spatial.astype(BF16), b_sp, w_branch_a.astype(BF16),
                  w_branch_b.astype(BF16), w_out.astype(BF16), row(norm_mix_post))
    return _mlp_call(h, row(norm_mlp_pre), w_up.astype(BF16), w_down.astype(BF16),
                     row(norm_mlp_post))


def kernel(x, norm_mix_pre, w_in, b_gate, ln_v_gain, ln_v_bias, w_spatial, b_spatial,
           w_branch_a, w_branch_b, w_out, norm_mix_post, norm_mlp_pre, w_up, w_down,
           norm_mlp_post):
    h = x
    for l in range(w_in.shape[0]):
        h = _layer(h, norm_mix_pre[l], w_in[l], b_gate[l], ln_v_gain[l], ln_v_bias[l],
                   w_spatial[l], b_spatial[l], w_branch_a[l], w_branch_b[l], w_out[l],
                   norm_mix_post[l], norm_mlp_pre[l], w_up[l], w_down[l], norm_mlp_post[l])
    return h
```

```python
import functools
import math

import jax
import jax.numpy as jnp
from jax import lax
from jax.experimental import pallas as pl
from jax.experimental.pallas import tpu as pltpu

GMLP_WIDTH = 512
GMLP_GROUPS = 4
CHUNK = 128
ATTN_PATTERNS = ((128, 1), (512, 4), (2048, 16))
N_GROUPS = len(ATTN_PATTERNS)
HEADS_PER_GROUP = 4
HEAD_DIM = 128
GROUP_WIDTH = HEADS_PER_GROUP * HEAD_DIM
ROPE_DIM = HEAD_DIM // 4
ROPE_THETA = 500000.0
NORM_EPS = 1e-6
MASK_VALUE = -1e30
LOG2E = math.log2(math.e)
RADIUS = 64
assert all(w // (2 * d) == RADIUS for w, d in ATTN_PATTERNS)
DEINT_STRIDE = 4
assert tuple(d for _, d in ATTN_PATTERNS) == (1, DEINT_STRIDE, DEINT_STRIDE ** 2)


def _slot_residue(slot, d):
    if d == DEINT_STRIDE ** 2:
        return DEINT_STRIDE * (slot % DEINT_STRIDE) + slot // DEINT_STRIDE
    return slot

LANES = 128
VMEM_LIMIT_BYTES = 56 * 1024 * 1024

TN = 512
PROLOGUE_ROWS = 256
PROJ_ROWS = 512
Q_BLOCK = 128
BLOCKS_PER_STEP = 4
N_BANDS = 3
COMBINE_ROWS = 256
MIX_ROWS = 512
MLP_ROWS = 512
MLP_FF_CHUNK = 1024

BF16 = jnp.bfloat16
F32 = jnp.float32

TILE_U, TILE_V, TILE_Q, TILE_K, TILE_AV, TILE_GATE = 0, 1, 2, 5, 8, 11


def _gelu_tanh(x):
    c = math.sqrt(2.0 / math.pi)
    return x * (0.5 * (1.0 + jnp.tanh(c * (x + 0.044715 * (x * x * x)))))


def _proj_kernel(x_ref, gain_ref, w_ref, bg_ref, lng_ref, lnb_ref, rope_ref, o_ref,
                 lhs_ref, slab_ref, *, seq, d_model):
    j = pl.program_id(1)
    n_slabs = d_model // LANES

    @pl.when(j == 0)
    def _prologue():
        st = DEINT_STRIDE
        cnt1 = PROLOGUE_ROWS // st
        cnt2 = cnt1 // st

        def body(c, carry):
            r0 = pl.multiple_of(c * PROLOGUE_ROWS, PROLOGUE_ROWS)
            xc = x_ref[pl.ds(r0, PROLOGUE_ROWS), :]
            ms = jnp.mean(xc * xc, axis=-1, keepdims=True)
            n = xc * lax.rsqrt(ms + NORM_EPS) * gain_ref[...]
            lhs_ref[0, pl.ds(r0, PROLOGUE_ROWS), :] = n.astype(BF16)
            for s in range(n_slabs):
                lanes = slice(s * LANES, (s + 1) * LANES)
                slab_ref[0, s] = n[:, lanes]
                for r1 in range(st):
                    piece = slab_ref[0, s, pl.ds(r1, cnt1, stride=st), :]
                    dst1 = pl.multiple_of(r1 * (seq // st) + c * cnt1, cnt1)
                    lhs_ref[1, pl.ds(dst1, cnt1), lanes] = piece.astype(BF16)
                    slab_ref[1, s, r1 * cnt1:(r1 + 1) * cnt1, :] = piece
                for r1 in range(st):
                    for r2 in range(st):
                        sub = slab_ref[1, s, pl.ds(r1 * cnt1 + r2, cnt2, stride=st), :]
                        slot = r1 * st + r2
                        dst2 = pl.multiple_of(slot * (seq // (st * st)) + c * cnt2, cnt2)
                        lhs_ref[2, pl.ds(dst2, cnt2), lanes] = sub.astype(BF16)
            return carry

        lax.fori_loop(0, seq // PROLOGUE_ROWS, body, 0)

    def matmul_rows(lhs_idx, epilogue):
        for c in range(seq // PROJ_ROWS):
            r0 = c * PROJ_ROWS
            acc = jnp.dot(lhs_ref[lhs_idx, pl.ds(r0, PROJ_ROWS), :], w_ref[...],
                          preferred_element_type=F32)
            o_ref[pl.ds(r0, PROJ_ROWS), :] = epilogue(acc, r0).astype(BF16)

    @pl.when(j == TILE_U)
    def _u():
        matmul_rows(0, lambda acc, r0: _gelu_tanh(acc))

    @pl.when(j == TILE_V)
    def _v():
        def epi(acc, r0):
            z = _gelu_tanh(acc)
            mu = jnp.mean(z, axis=-1, keepdims=True)
            zc = z - mu
            var = jnp.mean(zc * zc, axis=-1, keepdims=True)
            return zc * lax.rsqrt(var + NORM_EPS) * lng_ref[...] + lnb_ref[...]

        matmul_rows(0, epi)

    @pl.when((j >= TILE_Q) & (j < TILE_AV))
    def _qk():
        def epi(acc, r0):
            cos = rope_ref[0, pl.ds(r0, PROJ_ROWS), :]
            sin = rope_ref[1, pl.ds(r0, PROJ_ROWS), :]
            heads = []
            for h in range(HEADS_PER_GROUP):
                t = acc[:, h * HEAD_DIM:(h + 1) * HEAD_DIM]
                heads.append(t * cos + pltpu.roll(t, HEAD_DIM // 2, 1) * sin)
            return jnp.concatenate(heads, axis=1)

        matmul_rows((j - TILE_Q) % N_GROUPS, epi)

    @pl.when((j >= TILE_AV) & (j < TILE_GATE))
    def _av():
        matmul_rows(j - TILE_AV, lambda acc, r0: acc)

    @pl.when(j >= TILE_GATE)
    def _gate():
        matmul_rows(0, lambda acc, r0: 1.0 / (1.0 + jnp.exp(-(acc + bg_ref[...]))))


def _proj_call(x, gain, w_in, b_gate, ln_gain, ln_bias, rope_tab):
    bsz, seq, d_model = x.shape
    n_tiles = w_in.shape[1] // TN
    n_gate_tiles = n_tiles - TILE_GATE
    kern = functools.partial(_proj_kernel, seq=seq, d_model=d_model)
    return pl.pallas_call(
        kern,
        name="proj",
        grid=(bsz, n_tiles),
        in_specs=[
            pl.BlockSpec((None, seq, d_model), lambda b, j: (b, 0, 0)),
            pl.BlockSpec((1, d_model), lambda b, j: (0, 0)),
            pl.BlockSpec((d_model, TN), lambda b, j: (0, j)),
            pl.BlockSpec((1, TN), lambda b, j: (0, jnp.clip(j - TILE_GATE, 0, n_gate_tiles - 1))),
            pl.BlockSpec((1, TN), lambda b, j: (0, 0)),
            pl.BlockSpec((1, TN), lambda b, j: (0, 0)),
            pl.BlockSpec((None, 2, seq, HEAD_DIM),
                         lambda b, j: (jnp.clip(j - TILE_Q, 0, 2 * N_GROUPS - 1), 0, 0, 0)),
        ],
        out_specs=pl.BlockSpec((None, None, seq, TN), lambda b, j: (j, b, 0, 0)),
        out_shape=jax.ShapeDtypeStruct((n_tiles, bsz, seq, TN), BF16),
        scratch_shapes=[
            pltpu.VMEM((N_GROUPS, seq, d_model), BF16),
            pltpu.VMEM((2, d_model // LANES, PROLOGUE_ROWS, LANES), F32),
        ],
        compiler_params=pltpu.CompilerParams(
            dimension_semantics=("arbitrary", "arbitrary"),
            vmem_limit_bytes=VMEM_LIMIT_BYTES),
    )(x, gain, w_in, b_gate, ln_gain, ln_bias, rope_tab)


def _attn_group(q_ref, k_ref, v_ref, acc_slab, den_slab, max_slab, bias_ref, s_scr, p_scr,
                gi, d, seq):
    sub_len = seq // d
    n_blk = sub_len // Q_BLOCK
    win = min(2 * Q_BLOCK, sub_len)
    ones = jnp.ones((win, HEAD_DIM), BF16)

    def body(step, carry):
        blocks = []
        for e in range(BLOCKS_PER_STEP):
            f = step * BLOCKS_PER_STEP + e
            slot = f // n_blk
            q_off = (f % n_blk) * Q_BLOCK
            k_off = jnp.clip(q_off - RADIUS, 0, sub_len - win)
            blocks.append(dict(
                slot=slot, q_off=q_off, band=(q_off - k_off) // RADIUS,
                q0=pl.multiple_of(slot * sub_len + q_off, Q_BLOCK),
                k0=pl.multiple_of(slot * sub_len + k_off, RADIUS)))
        units = [(e, h) for e in range(BLOCKS_PER_STEP) for h in range(HEADS_PER_GROUP)]

        for u, (e, h) in enumerate(units):
            blk = blocks[e]
            hs = slice(h * HEAD_DIM, (h + 1) * HEAD_DIM)
            s_scr[u, :, :win] = lax.dot_general(
                q_ref[pl.ds(blk["q0"], Q_BLOCK), hs], k_ref[pl.ds(blk["k0"], win), hs],
                (((1,), (1,)), ((), ())), preferred_element_type=F32)

        row_max = []
        for u, (e, h) in enumerate(units):
            s = s_scr[u, :, :win] + bias_ref[blocks[e]["band"], :, :win]
            m = jnp.max(s, axis=-1, keepdims=True)
            p_scr[u, :, :win] = jnp.exp2(s - m).astype(BF16)
            row_max.append(m)

        for u, (e, h) in enumerate(units):
            blk = blocks[e]
            hs = slice(h * HEAD_DIM, (h + 1) * HEAD_DIM)
            v_ext = jnp.concatenate([v_ref[pl.ds(blk["k0"], win), hs], ones], axis=1)
            oe = jnp.dot(p_scr[u, :, :win], v_ext, preferred_element_type=F32)
            if d == 1:
                rows = pl.ds(blk["q0"], Q_BLOCK)
            else:
                rows = pl.ds(blk["q_off"] * d + _slot_residue(blk["slot"], d), Q_BLOCK, stride=d)
            idx = gi * HEADS_PER_GROUP + h
            acc_slab[idx, rows, :] = oe[:, :HEAD_DIM]
            den_slab[idx, rows, :] = oe[:, HEAD_DIM:]
            max_slab[idx, rows, :] = jnp.broadcast_to(row_max[u], (Q_BLOCK, HEAD_DIM))
        return carry

    lax.fori_loop(0, d * n_blk // BLOCKS_PER_STEP, body, 0)


def _attn_kernel(q_ref, k_ref, v_ref, y_ref, acc_slab, den_slab, max_slab, bias_ref,
                 s_scr, p_scr, *, seq):
    g = pl.program_id(1)

    @pl.when((pl.program_id(0) == 0) & (g == 0))
    def _band_bias():
        row = lax.broadcasted_iota(jnp.int32, (Q_BLOCK, 2 * Q_BLOCK), 0)
        col = lax.broadcasted_iota(jnp.int32, (Q_BLOCK, 2 * Q_BLOCK), 1)
        for band in range(N_BANDS):
            valid = jnp.abs(row - col + band * RADIUS) <= RADIUS
            bias_ref[band] = jnp.where(valid, 0.0, MASK_VALUE).astype(F32)

    for gi, (_, d) in enumerate(ATTN_PATTERNS):
        @pl.when(g == gi)
        def _group(gi=gi, d=d):
            _attn_group(q_ref, k_ref, v_ref, acc_slab, den_slab, max_slab, bias_ref,
                        s_scr, p_scr, gi, d, seq)

    @pl.when(g == N_GROUPS - 1)
    def _combine():
        def body(c, carry):
            rows = pl.ds(pl.multiple_of(c * COMBINE_ROWS, COMBINE_ROWS), COMBINE_ROWS)
            for h in range(HEADS_PER_GROUP):
                idx = [gi * HEADS_PER_GROUP + h for gi in range(N_GROUPS)]
                ms = [max_slab[i, rows, :] for i in idx]
                m = functools.reduce(jnp.maximum, ms)
                es = [jnp.exp2(mg - m) for mg in ms]
                den = functools.reduce(jnp.add, [e * den_slab[i, rows, :] for e, i in zip(es, idx)])
                num = functools.reduce(jnp.add, [e * acc_slab[i, rows, :] for e, i in zip(es, idx)])
                y_ref[rows, h * HEAD_DIM:(h + 1) * HEAD_DIM] = (num / den).astype(BF16)
            return carry

        lax.fori_loop(0, seq // COMBINE_ROWS, body, 0)


def _attn_call(proj):
    _, bsz, seq, _ = proj.shape
    n_slabs = N_GROUPS * HEADS_PER_GROUP

    def spec(first_tile):
        return pl.BlockSpec((None, None, seq, TN), lambda b, g: (first_tile + g, b, 0, 0))

    return pl.pallas_call(
        functools.partial(_attn_kernel, seq=seq),
        name="attn",
        grid=(bsz, N_GROUPS),
        in_specs=[spec(TILE_Q), spec(TILE_K), spec(TILE_AV)],
        out_specs=pl.BlockSpec((None, seq, GROUP_WIDTH), lambda b, g: (b, 0, 0)),
        out_shape=jax.ShapeDtypeStruct((bsz, seq, GROUP_WIDTH), BF16),
        scratch_shapes=[
            pltpu.VMEM((n_slabs, seq, LANES), F32),
            pltpu.VMEM((n_slabs, seq, LANES), F32),
            pltpu.VMEM((n_slabs, seq, LANES), F32),
            pltpu.VMEM((N_BANDS, Q_BLOCK, 2 * Q_BLOCK), F32),
            pltpu.VMEM((BLOCKS_PER_STEP * HEADS_PER_GROUP, Q_BLOCK, 2 * Q_BLOCK), F32),
            pltpu.VMEM((BLOCKS_PER_STEP * HEADS_PER_GROUP, Q_BLOCK, 2 * Q_BLOCK), BF16),
        ],
        compiler_params=pltpu.CompilerParams(
            dimension_semantics=("arbitrary", "arbitrary"),
            vmem_limit_bytes=VMEM_LIMIT_BYTES),
    )(proj, proj, proj)


def _mix_kernel(zu_ref, zv_ref, ga0_ref, ga1_ref, gb0_ref, gb1_ref, yb_ref, x_ref,
                ws_ref, bsp_ref, wa_ref, wb_ref, wo_ref, gain_ref, o_ref, ya_ref):
    group_dim = GMLP_WIDTH // GMLP_GROUPS
    for c in range(MIX_ROWS // CHUNK):
        rows = slice(c * CHUNK, (c + 1) * CHUNK)
        sv = jnp.concatenate(
            [jnp.dot(ws_ref[g], zv_ref[rows, g * group_dim:(g + 1) * group_dim],
                     preferred_element_type=F32) for g in range(GMLP_GROUPS)], axis=1)
        ya_ref[rows, :] = (zu_ref[rows, :].astype(F32) * (sv + bsp_ref[...])).astype(BF16)
    a = jnp.dot(ya_ref[...], wa_ref[...], preferred_element_type=F32)
    b = jnp.dot(yb_ref[...], wb_ref[...], preferred_element_type=F32)
    gate_a = jnp.concatenate([ga0_ref[...], ga1_ref[...]], axis=1).astype(F32)
    gate_b = jnp.concatenate([gb0_ref[...], gb1_ref[...]], axis=1).astype(F32)
    merged = gate_a * a + gate_b * b
    mix = jnp.dot(merged.astype(BF16), wo_ref[...], preferred_element_type=F32)
    ms = jnp.mean(mix * mix, axis=-1, keepdims=True)
    o_ref[...] = x_ref[...] + mix * lax.rsqrt(ms + NORM_EPS) * gain_ref[...]


def _mix_call(proj, y_b, x, w_spatial, b_sp, w_a, w_b, w_out, gain):
    bsz, seq, d_model = x.shape

    def tile(t):
        return pl.BlockSpec((None, None, MIX_ROWS, TN), lambda b, i: (t, b, i, 0))

    def whole(a):
        return pl.BlockSpec(a.shape, lambda b, i: (0,) * a.ndim)

    return pl.pallas_call(
        _mix_kernel,
        name="mix",
        grid=(bsz, seq // MIX_ROWS),
        in_specs=[
            tile(TILE_U), tile(TILE_V),
            tile(TILE_GATE), tile(TILE_GATE + 1), tile(TILE_GATE + 2), tile(TILE_GATE + 3),
            pl.BlockSpec((None, MIX_ROWS, GROUP_WIDTH), lambda b, i: (b, i, 0)),
            pl.BlockSpec((None, MIX_ROWS, d_model), lambda b, i: (b, i, 0)),
            whole(w_spatial), whole(b_sp), whole(w_a), whole(w_b), whole(w_out), whole(gain),
        ],
        out_specs=pl.BlockSpec((None, MIX_ROWS, d_model), lambda b, i: (b, i, 0)),
        out_shape=jax.ShapeDtypeStruct((bsz, seq, d_model), F32),
        scratch_shapes=[pltpu.VMEM((MIX_ROWS, GMLP_WIDTH), BF16)],
        compiler_params=pltpu.CompilerParams(
            dimension_semantics=("parallel", "parallel"),
            vmem_limit_bytes=VMEM_LIMIT_BYTES),
    )(proj, proj, proj, proj, proj, proj, y_b, x, w_spatial, b_sp, w_a, w_b, w_out, gain)


def _mlp_kernel(h_ref, gpre_ref, wu_ref, wd_ref, gpost_ref, o_ref, n_ref, acc_ref, *, d_ff):
    h = h_ref[...]
    ms = jnp.mean(h * h, axis=-1, keepdims=True)
    n_ref[...] = (h * lax.rsqrt(ms + NORM_EPS) * gpre_ref[...]).astype(BF16)
    for c in range(d_ff // MLP_FF_CHUNK):
        cols = slice(c * MLP_FF_CHUNK, (c + 1) * MLP_FF_CHUNK)
        up = jnp.dot(n_ref[...], wu_ref[:, cols], preferred_element_type=F32)
        hid = jnp.square(jnp.maximum(up, 0.0)).astype(BF16)
        part = jnp.dot(hid, wd_ref[cols, :], preferred_element_type=F32)
        if c == 0:
            acc_ref[...] = part
        else:
            acc_ref[...] += part
    out = acc_ref[...]
    ms2 = jnp.mean(out * out, axis=-1, keepdims=True)
    o_ref[...] = h_ref[...] + out * lax.rsqrt(ms2 + NORM_EPS) * gpost_ref[...]


def _mlp_call(h, gain_pre, w_up, w_down, gain_post):
    bsz, seq, d_model = h.shape
    d_ff = w_up.shape[1]
    rows = bsz * seq
    h2 = h.reshape(rows, d_model)

    def whole(a):
        return pl.BlockSpec(a.shape, lambda i: (0,) * a.ndim, pipeline_mode=pl.Buffered(1))

    out = pl.pallas_call(
        functools.partial(_mlp_kernel, d_ff=d_ff),
        name="mlp",
        grid=(rows // MLP_ROWS,),
        in_specs=[
            pl.BlockSpec((MLP_ROWS, d_model), lambda i: (i, 0)),
            whole(gain_pre), whole(w_up), whole(w_down), whole(gain_post),
        ],
        out_specs=pl.BlockSpec((MLP_ROWS, d_model), lambda i: (i, 0)),
        out_shape=jax.ShapeDtypeStruct((rows, d_model), F32),
        scratch_shapes=[
            pltpu.VMEM((MLP_ROWS, d_model), BF16),
            pltpu.VMEM((MLP_ROWS, d_model), F32),
        ],
        compiler_params=pltpu.CompilerParams(
            dimension_semantics=("parallel",),
            vmem_limit_bytes=VMEM_LIMIT_BYTES),
    )(h2, gain_pre, w_up, w_down, gain_post)
    return out.reshape(bsz, seq, d_model)


def _rope_lane_order():
    half = ROPE_DIM // 2
    fill = HEAD_DIM // 2 - half
    return (list(range(half)) + list(range(ROPE_DIM, ROPE_DIM + fill))
            + list(range(half, ROPE_DIM)) + list(range(ROPE_DIM + fill, HEAD_DIM)))


def _permute_qk_columns(w_in):
    n_qk_heads = 2 * N_GROUPS * HEADS_PER_GROUP
    q_start = 2 * GMLP_WIDTH
    q_end = q_start + n_qk_heads * HEAD_DIM
    order = _rope_lane_order()
    runs, start = [], 0
    for i in range(1, HEAD_DIM + 1):
        if i == HEAD_DIM or order[i] != order[i - 1] + 1:
            runs.append((order[start], order[i - 1] + 1))
            start = i
    qk = w_in[:, q_start:q_end].reshape(w_in.shape[0], n_qk_heads, HEAD_DIM)
    qk = jnp.concatenate([qk[..., a:b] for a, b in runs], axis=-1)
    return jnp.concatenate(
        [w_in[:, :q_start], qk.reshape(w_in.shape[0], -1), w_in[:, q_end:]], axis=1)


def _rope_tables(seq):
    half = ROPE_DIM // 2
    fill = HEAD_DIM // 2 - half
    inv_freq = ROPE_THETA ** (-jnp.arange(0, ROPE_DIM, 2, dtype=F32) / ROPE_DIM)
    ang = jnp.arange(seq, dtype=F32)[:, None] * inv_freq[None, :]
    cos, sin = jnp.cos(ang), jnp.sin(ang)
    ones = jnp.ones((seq, fill), F32)
    cos_l = jnp.concatenate([cos, ones, cos, ones], axis=1)
    sin_l = jnp.concatenate([-sin, 0.0 * ones, sin, 0.0 * ones], axis=1)
    nat = jnp.stack([cos_l, sin_l])
    per_group = []
    for _, d in ATTN_PATTERNS:
        by_residue = nat.reshape(2, seq // d, d, HEAD_DIM).transpose(0, 2, 1, 3)
        residues = jnp.asarray([_slot_residue(slot, d) for slot in range(d)], jnp.int32)
        per_group.append(by_residue[:, residues].reshape(2, seq, HEAD_DIM))
    k_tab = jnp.stack(per_group)
    q_tab = k_tab * (LOG2E / math.sqrt(HEAD_DIM))
    return jnp.concatenate([q_tab, k_tab], axis=0)


def _layer(h, norm_mix_pre, w_in, b_gate, ln_v_gain, ln_v_bias, w_spatial, b_spatial,
           w_branch_a, w_branch_b, w_out, norm_mix_post, norm_mlp_pre, w_up, w_down,
           norm_mlp_post):
    seq = h.shape[1]
    row = lambda v: v.reshape(1, -1).astype(F32)
    proj = _proj_call(h, row(norm_mix_pre), _permute_qk_columns(w_in).astype(BF16), row(b_gate), row(ln_v_gain),
                      row(ln_v_bias), _rope_tables(seq))
    y_b = _attn_call(proj)
    b_sp = jnp.repeat(b_spatial.T.astype(F32), GMLP_WIDTH // GMLP_GROUPS, axis=1)
    h = _mix_call(proj, y_b, h, w_spatial.astype(BF16), b_sp, w_branch_a.astype(BF16),
                  w_branch_b.astype(BF16), w_out.astype(BF16), row(norm_mix_post))
    return _mlp_call(h, row(norm_mlp_pre), w_up.astype(BF16), w_down.astype(BF16),
                     row(norm_mlp_post))


def kernel(x, norm_mix_pre, w_in, b_gate, ln_v_gain, ln_v_bias, w_spatial, b_spatial,
           w_branch_a, w_branch_b, w_out, norm_mix_post, norm_mlp_pre, w_up, w_down,
           norm_mlp_post):
    h = x
    for l in range(w_in.shape[0]):
        h = _layer(h, norm_mix_pre[l], w_in[l], b_gate[l], ln_v_gain[l], ln_v_bias[l],
                   w_spatial[l], b_spatial[l], w_branch_a[l], w_branch_b[l], w_out[l],
                   norm_mix_post[l], norm_mlp_pre[l], w_up[l], w_down[l], norm_mlp_post[l])
    return h
```

```python
import functools
import math

import jax
import jax.numpy as jnp
import numpy as np
from jax import lax
from jax.experimental import pallas as pl
from jax.experimental.pallas import tpu as pltpu

GMLP_WIDTH = 512
GMLP_GROUPS = 4
CHUNK = 128
ATTN_PATTERNS = ((128, 1), (512, 4), (2048, 16))
N_GROUPS = len(ATTN_PATTERNS)
HEADS_PER_GROUP = 4
HEAD_DIM = 128
GROUP_WIDTH = HEADS_PER_GROUP * HEAD_DIM
ROPE_DIM = HEAD_DIM // 4
ROPE_THETA = 500000.0
NORM_EPS = 1e-6
MASK_VALUE = -1e30
LOG2E = math.log2(math.e)
QK_SCALE = LOG2E / math.sqrt(HEAD_DIM)
RADIUS = 64
assert all(w // (2 * d) == RADIUS for w, d in ATTN_PATTERNS)
DEINT_STRIDE = 4
assert tuple(d for _, d in ATTN_PATTERNS) == (1, DEINT_STRIDE, DEINT_STRIDE ** 2)


def _slot_residue(slot, d):
    if d == DEINT_STRIDE ** 2:
        return DEINT_STRIDE * (slot % DEINT_STRIDE) + slot // DEINT_STRIDE
    return slot


LANES = 128
VMEM_LIMIT_BYTES = 56 * 1024 * 1024

PIECE = 512
PIECES_PER_TILE = 3
TN = PIECES_PER_TILE * PIECE
PROLOGUE_ROWS = 256
PROJ_ROWS = 512
Q_BLOCK = 128
BLOCKS_PER_STEP = 4
N_BANDS = 3
COMBINE_ROWS = 256
MIX_ROWS = 512
MLP_ROWS = 512
MLP_FF_CHUNK = 1024

BF16 = jnp.bfloat16
F32 = jnp.float32

TILE_A, TILE_B, TILE_QKV = 0, 1, 2
N_TILES = TILE_QKV + N_GROUPS


def _gelu_tanh(x):
    c = math.sqrt(2.0 / math.pi)
    return x * (0.5 * (1.0 + jnp.tanh(c * (x + 0.044715 * (x * x * x)))))


def _sigmoid(x):
    return 1.0 / (1.0 + jnp.exp(-x))


def _proj_kernel(x_ref, gain_ref, w_ref, bg_ref, lng_ref, lnb_ref, rope_ref, o_ref,
                 lhs_ref, slab_ref, *, seq, d_model):
    j = pl.program_id(1)
    n_slabs = d_model // LANES

    @pl.when(j == 0)
    def _prologue():
        st = DEINT_STRIDE
        cnt1 = PROLOGUE_ROWS // st
        cnt2 = cnt1 // st

        def body(c, carry):
            r0 = pl.multiple_of(c * PROLOGUE_ROWS, PROLOGUE_ROWS)
            xc = x_ref[pl.ds(r0, PROLOGUE_ROWS), :]
            ms = jnp.mean(xc * xc, axis=-1, keepdims=True)
            n = xc * lax.rsqrt(ms + NORM_EPS) * gain_ref[...]
            lhs_ref[0, pl.ds(r0, PROLOGUE_ROWS), :] = n.astype(BF16)
            for s in range(n_slabs):
                lanes = slice(s * LANES, (s + 1) * LANES)
                slab_ref[0, s] = n[:, lanes]
                for r1 in range(st):
                    piece = slab_ref[0, s, pl.ds(r1, cnt1, stride=st), :]
                    dst1 = pl.multiple_of(r1 * (seq // st) + c * cnt1, cnt1)
                    lhs_ref[1, pl.ds(dst1, cnt1), lanes] = piece.astype(BF16)
                    slab_ref[1, s, r1 * cnt1:(r1 + 1) * cnt1, :] = piece
                for r1 in range(st):
                    for r2 in range(st):
                        sub = slab_ref[1, s, pl.ds(r1 * cnt1 + r2, cnt2, stride=st), :]
                        slot = r1 * st + r2
                        dst2 = pl.multiple_of(slot * (seq // (st * st)) + c * cnt2, cnt2)
                        lhs_ref[2, pl.ds(dst2, cnt2), lanes] = sub.astype(BF16)
            return carry

        lax.fori_loop(0, seq // PROLOGUE_ROWS, body, 0)

    def matmul_rows(lhs_idx, epilogues):
        for c in range(seq // PROJ_ROWS):
            r0 = c * PROJ_ROWS
            acc = jnp.dot(lhs_ref[lhs_idx, pl.ds(r0, PROJ_ROWS), :], w_ref[...],
                          preferred_element_type=F32)
            for c0, width, fn in epilogues:
                o_ref[pl.ds(r0, PROJ_ROWS), c0:c0 + width] = (
                    fn(acc[:, c0:c0 + width], r0).astype(BF16))

    def gates(acc, r0):
        return _sigmoid(acc + bg_ref[...])

    def layer_norm_gelu(acc, r0):
        z = _gelu_tanh(acc)
        mu = jnp.mean(z, axis=-1, keepdims=True)
        zc = z - mu
        var = jnp.mean(zc * zc, axis=-1, keepdims=True)
        return zc * lax.rsqrt(var + NORM_EPS) * lng_ref[...] + lnb_ref[...]

    def rotary(scale):
        def fn(acc, r0):
            cos = rope_ref[0, pl.ds(r0, PROJ_ROWS), :]
            sin = rope_ref[1, pl.ds(r0, PROJ_ROWS), :]
            heads = []
            for h in range(HEADS_PER_GROUP):
                t = acc[:, h * HEAD_DIM:(h + 1) * HEAD_DIM]
                out = t * cos + pltpu.roll(t, HEAD_DIM // 2, 1) * sin
                heads.append(out if scale is None else out * scale)
            return jnp.concatenate(heads, axis=1)

        return fn

    @pl.when(j == TILE_A)
    def _tile_a():
        matmul_rows(0, [(0, 2 * PIECE, gates),
                        (2 * PIECE, PIECE, lambda acc, r0: _gelu_tanh(acc))])

    @pl.when(j == TILE_B)
    def _tile_b():
        matmul_rows(0, [(0, 2 * PIECE, gates), (2 * PIECE, PIECE, layer_norm_gelu)])

    @pl.when(j >= TILE_QKV)
    def _tile_qkv():
        matmul_rows(j - TILE_QKV, [(0, PIECE, rotary(QK_SCALE)),
                                   (PIECE, PIECE, rotary(None)),
                                   (2 * PIECE, PIECE, lambda acc, r0: acc)])


def _proj_call(x, gain, w_in, b_gate, ln_gain, ln_bias, rope_tab):
    bsz, seq, d_model = x.shape
    assert w_in.shape[1] == N_TILES * TN
    kern = functools.partial(_proj_kernel, seq=seq, d_model=d_model)
    return pl.pallas_call(
        kern,
        name="proj",
        grid=(bsz, N_TILES),
        in_specs=[
            pl.BlockSpec((None, seq, d_model), lambda b, j: (b, 0, 0)),
            pl.BlockSpec((1, d_model), lambda b, j: (0, 0)),
            pl.BlockSpec((d_model, TN), lambda b, j: (0, j)),
            pl.BlockSpec((1, 2 * PIECE), lambda b, j: (0, jnp.minimum(j, TILE_B))),
            pl.BlockSpec((1, PIECE), lambda b, j: (0, 0)),
            pl.BlockSpec((1, PIECE), lambda b, j: (0, 0)),
            pl.BlockSpec((None, 2, seq, HEAD_DIM),
                         lambda b, j: (jnp.maximum(j - TILE_QKV, 0), 0, 0, 0)),
        ],
        out_specs=pl.BlockSpec((None, None, seq, TN), lambda b, j: (j, b, 0, 0)),
        out_shape=jax.ShapeDtypeStruct((N_TILES, bsz, seq, TN), BF16),
        scratch_shapes=[
            pltpu.VMEM((N_GROUPS, seq, d_model), BF16),
            pltpu.VMEM((2, d_model // LANES, PROLOGUE_ROWS, LANES), F32),
        ],
        compiler_params=pltpu.CompilerParams(
            dimension_semantics=("arbitrary", "arbitrary"),
            vmem_limit_bytes=VMEM_LIMIT_BYTES),
    )(x, gain, w_in, b_gate, ln_gain, ln_bias, rope_tab)


def _attn_group(q_ref, k_ref, v_ref, acc_slab, den_slab, max_slab, bias_ref, s_scr, p_scr,
                gi, d, seq):
    sub_len = seq // d
    n_blk = sub_len // Q_BLOCK
    win = min(2 * Q_BLOCK, sub_len)
    ones = jnp.ones((win, HEAD_DIM), BF16)

    def body(step, carry):
        blocks = []
        for e in range(BLOCKS_PER_STEP):
            f = step * BLOCKS_PER_STEP + e
            slot = f // n_blk
            q_off = (f % n_blk) * Q_BLOCK
            k_off = jnp.clip(q_off - RADIUS, 0, sub_len - win)
            blocks.append(dict(
                slot=slot, q_off=q_off, band=(q_off - k_off) // RADIUS,
                q0=pl.multiple_of(slot * sub_len + q_off, Q_BLOCK),
                k0=pl.multiple_of(slot * sub_len + k_off, RADIUS)))
        units = [(e, h) for e in range(BLOCKS_PER_STEP) for h in range(HEADS_PER_GROUP)]

        for u, (e, h) in enumerate(units):
            blk = blocks[e]
            hs = slice(h * HEAD_DIM, (h + 1) * HEAD_DIM)
            s_scr[u, :, :win] = lax.dot_general(
                q_ref[pl.ds(blk["q0"], Q_BLOCK), hs], k_ref[pl.ds(blk["k0"], win), hs],
                (((1,), (1,)), ((), ())), preferred_element_type=F32)

        row_max = []
        for u, (e, h) in enumerate(units):
            s = s_scr[u, :, :win] + bias_ref[blocks[e]["band"], :, :win]
            m = jnp.max(s, axis=-1, keepdims=True)
            p_scr[u, :, :win] = jnp.exp2(s - m).astype(BF16)
            row_max.append(m)

        for u, (e, h) in enumerate(units):
            blk = blocks[e]
            hs = slice(h * HEAD_DIM, (h + 1) * HEAD_DIM)
            v_ext = jnp.concatenate([v_ref[pl.ds(blk["k0"], win), hs], ones], axis=1)
            oe = jnp.dot(p_scr[u, :, :win], v_ext, preferred_element_type=F32)
            if d == 1:
                rows = pl.ds(blk["q0"], Q_BLOCK)
            else:
                rows = pl.ds(blk["q_off"] * d + _slot_residue(blk["slot"], d), Q_BLOCK, stride=d)
            idx = gi * HEADS_PER_GROUP + h
            acc_slab[idx, rows, :] = oe[:, :HEAD_DIM]
            den_slab[idx, rows, :] = oe[:, HEAD_DIM:]
            max_slab[idx, rows, :] = jnp.broadcast_to(row_max[u], (Q_BLOCK, HEAD_DIM))
        return carry

    lax.fori_loop(0, d * n_blk // BLOCKS_PER_STEP, body, 0)


def _attn_kernel(q_ref, k_ref, v_ref, y_ref, acc_slab, den_slab, max_slab, bias_ref,
                 s_scr, p_scr, *, seq):
    g = pl.program_id(1)

    @pl.when((pl.program_id(0) == 0) & (g == 0))
    def _band_bias():
        row = lax.broadcasted_iota(jnp.int32, (Q_BLOCK, 2 * Q_BLOCK), 0)
        col = lax.broadcasted_iota(jnp.int32, (Q_BLOCK, 2 * Q_BLOCK), 1)
        for band in range(N_BANDS):
            valid = jnp.abs(row - col + band * RADIUS) <= RADIUS
            bias_ref[band] = jnp.where(valid, 0.0, MASK_VALUE).astype(F32)

    for gi, (_, d) in enumerate(ATTN_PATTERNS):
        @pl.when(g == gi)
        def _group(gi=gi, d=d):
            _attn_group(q_ref, k_ref, v_ref, acc_slab, den_slab, max_slab, bias_ref,
                        s_scr, p_scr, gi, d, seq)

    @pl.when(g == N_GROUPS - 1)
    def _combine():
        def body(c, carry):
            rows = pl.ds(pl.multiple_of(c * COMBINE_ROWS, COMBINE_ROWS), COMBINE_ROWS)
            for h in range(HEADS_PER_GROUP):
                idx = [gi * HEADS_PER_GROUP + h for gi in range(N_GROUPS)]
                ms = [max_slab[i, rows, :] for i in idx]
                m = functools.reduce(jnp.maximum, ms)
                es = [jnp.exp2(mg - m) for mg in ms]
                den = functools.reduce(jnp.add, [e * den_slab[i, rows, :] for e, i in zip(es, idx)])
                num = functools.reduce(jnp.add, [e * acc_slab[i, rows, :] for e, i in zip(es, idx)])
                y_ref[rows, h * HEAD_DIM:(h + 1) * HEAD_DIM] = (num / den).astype(BF16)
            return carry

        lax.fori_loop(0, seq // COMBINE_ROWS, body, 0)


def _attn_call(proj):
    _, bsz, seq, _ = proj.shape
    n_slabs = N_GROUPS * HEADS_PER_GROUP

    def spec(piece):
        return pl.BlockSpec((None, None, seq, PIECE), lambda b, g: (TILE_QKV + g, b, 0, piece))

    return pl.pallas_call(
        functools.partial(_attn_kernel, seq=seq),
        name="attn",
        grid=(bsz, N_GROUPS),
        in_specs=[spec(0), spec(1), spec(2)],
        out_specs=pl.BlockSpec((None, seq, GROUP_WIDTH), lambda b, g: (b, 0, 0)),
        out_shape=jax.ShapeDtypeStruct((bsz, seq, GROUP_WIDTH), BF16),
        scratch_shapes=[
            pltpu.VMEM((n_slabs, seq, LANES), F32),
            pltpu.VMEM((n_slabs, seq, LANES), F32),
            pltpu.VMEM((n_slabs, seq, LANES), F32),
            pltpu.VMEM((N_BANDS, Q_BLOCK, 2 * Q_BLOCK), F32),
            pltpu.VMEM((BLOCKS_PER_STEP * HEADS_PER_GROUP, Q_BLOCK, 2 * Q_BLOCK), F32),
            pltpu.VMEM((BLOCKS_PER_STEP * HEADS_PER_GROUP, Q_BLOCK, 2 * Q_BLOCK), BF16),
        ],
        compiler_params=pltpu.CompilerParams(
            dimension_semantics=("arbitrary", "arbitrary"),
            vmem_limit_bytes=VMEM_LIMIT_BYTES),
    )(proj, proj, proj)


def _mix_kernel(zu_ref, zv_ref, ga_ref, gb_ref, yb_ref, x_ref,
                ws_ref, bsp_ref, wa_ref, wb_ref, wo_ref, gain_ref, o_ref, ya_ref):
    group_dim = GMLP_WIDTH // GMLP_GROUPS
    for c in range(MIX_ROWS // CHUNK):
        rows = slice(c * CHUNK, (c + 1) * CHUNK)
        sv = jnp.concatenate(
            [jnp.dot(ws_ref[g], zv_ref[rows, g * group_dim:(g + 1) * group_dim],
                     preferred_element_type=F32) for g in range(GMLP_GROUPS)], axis=1)
        ya_ref[rows, :] = (zu_ref[rows, :].astype(F32) * (sv + bsp_ref[...])).astype(BF16)
    a = jnp.dot(ya_ref[...], wa_ref[...], preferred_element_type=F32)
    b = jnp.dot(yb_ref[...], wb_ref[...], preferred_element_type=F32)
    merged = ga_ref[...].astype(F32) * a + gb_ref[...].astype(F32) * b
    mix = jnp.dot(merged.astype(BF16), wo_ref[...], preferred_element_type=F32)
    ms = jnp.mean(mix * mix, axis=-1, keepdims=True)
    o_ref[...] = x_ref[...] + mix * lax.rsqrt(ms + NORM_EPS) * gain_ref[...]


def _mix_call(proj, y_b, x, w_spatial, b_sp, w_a, w_b, w_out, gain):
    bsz, seq, d_model = x.shape

    def piece(tile, first_piece, n_pieces):
        width = n_pieces * PIECE
        return pl.BlockSpec((None, None, MIX_ROWS, width),
                            lambda b, i: (tile, b, i, first_piece // n_pieces))

    def whole(a):
        return pl.BlockSpec(a.shape, lambda b, i: (0,) * a.ndim)

    return pl.pallas_call(
        _mix_kernel,
        name="mix",
        grid=(bsz, seq // MIX_ROWS),
        in_specs=[
            piece(TILE_A, 2, 1), piece(TILE_B, 2, 1), piece(TILE_A, 0, 2), piece(TILE_B, 0, 2),
            pl.BlockSpec((None, MIX_ROWS, GROUP_WIDTH), lambda b, i: (b, i, 0)),
            pl.BlockSpec((None, MIX_ROWS, d_model), lambda b, i: (b, i, 0)),
            whole(w_spatial), whole(b_sp), whole(w_a), whole(w_b), whole(w_out), whole(gain),
        ],
        out_specs=pl.BlockSpec((None, MIX_ROWS, d_model), lambda b, i: (b, i, 0)),
        out_shape=jax.ShapeDtypeStruct((bsz, seq, d_model), F32),
        scratch_shapes=[pltpu.VMEM((MIX_ROWS, GMLP_WIDTH), BF16)],
        compiler_params=pltpu.CompilerParams(
            dimension_semantics=("parallel", "parallel"),
            vmem_limit_bytes=VMEM_LIMIT_BYTES),
    )(proj, proj, proj, proj, y_b, x, w_spatial, b_sp, w_a, w_b, w_out, gain)


def _mlp_kernel(h_ref, gpre_ref, wu_ref, wd_ref, gpost_ref, o_ref, n_ref, acc_ref, *, d_ff):
    h = h_ref[...]
    ms = jnp.mean(h * h, axis=-1, keepdims=True)
    n_ref[...] = (h * lax.rsqrt(ms + NORM_EPS) * gpre_ref[...]).astype(BF16)
    for c in range(d_ff // MLP_FF_CHUNK):
        cols = slice(c * MLP_FF_CHUNK, (c + 1) * MLP_FF_CHUNK)
        up = jnp.dot(n_ref[...], wu_ref[:, cols], preferred_element_type=F32)
        hid = jnp.square(jnp.maximum(up, 0.0)).astype(BF16)
        part = jnp.dot(hid, wd_ref[cols, :], preferred_element_type=F32)
        if c == 0:
            acc_ref[...] = part
        else:
            acc_ref[...] += part
    out = acc_ref[...]
    ms2 = jnp.mean(out * out, axis=-1, keepdims=True)
    o_ref[...] = h_ref[...] + out * lax.rsqrt(ms2 + NORM_EPS) * gpost_ref[...]


def _mlp_call(h, gain_pre, w_up, w_down, gain_post):
    bsz, seq, d_model = h.shape
    d_ff = w_up.shape[1]
    rows = bsz * seq
    h2 = h.reshape(rows, d_model)

    def whole(a):
        return pl.BlockSpec(a.shape, lambda i: (0,) * a.ndim, pipeline_mode=pl.Buffered(1))

    out = pl.pallas_call(
        functools.partial(_mlp_kernel, d_ff=d_ff),
        name="mlp",
        grid=(rows // MLP_ROWS,),
        in_specs=[
            pl.BlockSpec((MLP_ROWS, d_model), lambda i: (i, 0)),
            whole(gain_pre), whole(w_up), whole(w_down), whole(gain_post),
        ],
        out_specs=pl.BlockSpec((MLP_ROWS, d_model), lambda i: (i, 0)),
        out_shape=jax.ShapeDtypeStruct((rows, d_model), F32),
        scratch_shapes=[
            pltpu.VMEM((MLP_ROWS, d_model), BF16),
            pltpu.VMEM((MLP_ROWS, d_model), F32),
        ],
        compiler_params=pltpu.CompilerParams(
            dimension_semantics=("parallel",),
            vmem_limit_bytes=VMEM_LIMIT_BYTES),
    )(h2, gain_pre, w_up, w_down, gain_post)
    return out.reshape(bsz, seq, d_model)


def _rope_lane_order():
    half = ROPE_DIM // 2
    fill = HEAD_DIM // 2 - half
    return (list(range(half)) + list(range(ROPE_DIM, ROPE_DIM + fill))
            + list(range(half, ROPE_DIM)) + list(range(ROPE_DIM + fill, HEAD_DIM)))


def _regroup_w_in(w_in):
    d_model = w_in.shape[0]
    w = w_in.astype(BF16)
    gm = 2 * GMLP_WIDTH
    aw = N_GROUPS * GROUP_WIDTH
    select = np.zeros((HEAD_DIM, HEAD_DIM), np.float32)
    select[_rope_lane_order(), np.arange(HEAD_DIM)] = 1.0
    qk = jnp.dot(w[:, gm:gm + 2 * aw].reshape(-1, HEAD_DIM), jnp.asarray(select, BF16),
                 preferred_element_type=BF16).reshape(d_model, 2 * aw)

    def attn_piece(section, g):
        if section < 2:
            return qk[:, section * aw + g * PIECE:section * aw + (g + 1) * PIECE]
        return w[:, gm + 2 * aw + g * PIECE:gm + 2 * aw + (g + 1) * PIECE]

    gates = w[:, gm + 3 * aw:]
    pieces = [gates[:, :2 * PIECE], w[:, :PIECE], gates[:, 2 * PIECE:], w[:, PIECE:gm]]
    for g in range(N_GROUPS):
        pieces += [attn_piece(s, g) for s in range(3)]
    return jnp.concatenate(pieces, axis=1)


def _rope_tables(seq):
    half = ROPE_DIM // 2
    fill = HEAD_DIM // 2 - half
    inv_freq = np.float32(ROPE_THETA) ** (-np.arange(0, ROPE_DIM, 2, dtype=np.float32) / ROPE_DIM)
    ang = np.arange(seq, dtype=np.float32)[:, None] * inv_freq[None, :].astype(np.float32)
    cos, sin = np.cos(ang), np.sin(ang)
    ones = np.ones((seq, fill), np.float32)
    cos_l = np.concatenate([cos, ones, cos, ones], axis=1)
    sin_l = np.concatenate([-sin, 0.0 * ones, sin, 0.0 * ones], axis=1)
    nat = np.stack([cos_l, sin_l])
    per_group = []
    for _, d in ATTN_PATTERNS:
        by_residue = nat.reshape(2, seq // d, d, HEAD_DIM).transpose(0, 2, 1, 3)
        residues = [_slot_residue(slot, d) for slot in range(d)]
        per_group.append(by_residue[:, residues].reshape(2, seq, HEAD_DIM))
    return jnp.asarray(np.stack(per_group), F32)


def _layer(h, norm_mix_pre, w_in, b_gate, ln_v_gain, ln_v_bias, w_spatial, b_spatial,
           w_branch_a, w_branch_b, w_out, norm_mix_post, norm_mlp_pre, w_up, w_down,
           norm_mlp_post):
    seq = h.shape[1]
    row = lambda v: v.reshape(1, -1).astype(F32)
    proj = _proj_call(h, row(norm_mix_pre), _regroup_w_in(w_in), row(b_gate), row(ln_v_gain),
                      row(ln_v_bias), _rope_tables(seq))
    y_b = _attn_call(proj)
    b_sp = jnp.repeat(b_spatial.T.astype(F32), GMLP_WIDTH // GMLP_GROUPS, axis=1)
    h = _mix_call(proj, y_b, h, w_spatial.astype(BF16), b_sp, w_branch_a.astype(BF16),
                  w_branch_b.astype(BF16), w_out.astype(BF16), row(norm_mix_post))
    return _mlp_call(h, row(norm_mlp_pre), w_up.astype(BF16), w_down.astype(BF16),
                     row(norm_mlp_post))


def kernel(x, norm_mix_pre, w_in, b_gate, ln_v_gain, ln_v_bias, w_spatial, b_spatial,
           w_branch_a, w_branch_b, w_out, norm_mix_post, norm_mlp_pre, w_up, w_down,
           norm_mlp_post):
    h = x
    for l in range(w_in.shape[0]):
        h = _layer(h, norm_mix_pre[l], w_in[l], b_gate[l], ln_v_gain[l], ln_v_bias[l],
                   w_spatial[l], b_spatial[l], w_branch_a[l], w_branch_b[l], w_out[l],
                   norm_mix_post[l], norm_mlp_pre[l], w_up[l], w_down[l], norm_mlp_post[l])
    return h
```

```python
import functools
import math

import jax
import jax.numpy as jnp
import numpy as np
from jax import lax
from jax.experimental import pallas as pl
from jax.experimental.pallas import tpu as pltpu

GMLP_WIDTH = 512
GMLP_GROUPS = 4
CHUNK = 128
ATTN_PATTERNS = ((128, 1), (512, 4), (2048, 16))
N_GROUPS = len(ATTN_PATTERNS)
HEADS_PER_GROUP = 4
HEAD_DIM = 128
GROUP_WIDTH = HEADS_PER_GROUP * HEAD_DIM
ROPE_DIM = HEAD_DIM // 4
ROPE_THETA = 500000.0
NORM_EPS = 1e-6
MASK_VALUE = -1e30
LOG2E = math.log2(math.e)
QK_SCALE = LOG2E / math.sqrt(HEAD_DIM)
RADIUS = 64
assert all(w // (2 * d) == RADIUS for w, d in ATTN_PATTERNS)
DEINT_STRIDE = 4
assert tuple(d for _, d in ATTN_PATTERNS) == (1, DEINT_STRIDE, DEINT_STRIDE ** 2)


def _slot_residue(slot, d):
    if d == DEINT_STRIDE ** 2:
        return DEINT_STRIDE * (slot % DEINT_STRIDE) + slot // DEINT_STRIDE
    return slot


LANES = 128
VMEM_LIMIT_BYTES = 56 * 1024 * 1024

PIECE = 512
PIECES_PER_TILE = 3
TN = PIECES_PER_TILE * PIECE
PROLOGUE_ROWS = 256
PROJ_ROWS = 256
Q_BLOCK = 128
BLOCKS_PER_STEP = 4
N_BANDS = 3
COMBINE_ROWS = 256
MIX_ROWS = 512
MLP_ROWS = 1024
MLP_FF_CHUNK = 1024

BF16 = jnp.bfloat16
F32 = jnp.float32

TILE_A, TILE_B, TILE_QKV = 0, 1, 2
N_TILES = TILE_QKV + N_GROUPS


def _gelu_tanh(x):
    c = math.sqrt(2.0 / math.pi)
    return x * (0.5 * (1.0 + jnp.tanh(c * (x + 0.044715 * (x * x * x)))))


def _sigmoid(x):
    return 1.0 / (1.0 + jnp.exp(-x))


def _proj_kernel(x_ref, gain_ref, w_ref, bg_ref, lng_ref, lnb_ref, rope_ref, o_ref,
                 lhs_ref, slab_ref, *, seq, d_model):
    j = pl.program_id(1)
    n_slabs = d_model // LANES

    @pl.when(j == 0)
    def _prologue():
        st = DEINT_STRIDE
        cnt1 = PROLOGUE_ROWS // st
        cnt2 = cnt1 // st

        def body(c, carry):
            r0 = pl.multiple_of(c * PROLOGUE_ROWS, PROLOGUE_ROWS)
            xc = x_ref[pl.ds(r0, PROLOGUE_ROWS), :]
            ms = jnp.mean(xc * xc, axis=-1, keepdims=True)
            n = xc * lax.rsqrt(ms + NORM_EPS) * gain_ref[...]
            lhs_ref[0, pl.ds(r0, PROLOGUE_ROWS), :] = n.astype(BF16)
            for s in range(n_slabs):
                lanes = slice(s * LANES, (s + 1) * LANES)
                slab_ref[0, s] = n[:, lanes]
                for r1 in range(st):
                    piece = slab_ref[0, s, pl.ds(r1, cnt1, stride=st), :]
                    dst1 = pl.multiple_of(r1 * (seq // st) + c * cnt1, cnt1)
                    lhs_ref[1, pl.ds(dst1, cnt1), lanes] = piece.astype(BF16)
                    slab_ref[1, s, r1 * cnt1:(r1 + 1) * cnt1, :] = piece
                for r1 in range(st):
                    for r2 in range(st):
                        sub = slab_ref[1, s, pl.ds(r1 * cnt1 + r2, cnt2, stride=st), :]
                        slot = r1 * st + r2
                        dst2 = pl.multiple_of(slot * (seq // (st * st)) + c * cnt2, cnt2)
                        lhs_ref[2, pl.ds(dst2, cnt2), lanes] = sub.astype(BF16)
            return carry

        lax.fori_loop(0, seq // PROLOGUE_ROWS, body, 0)

    def matmul_rows(lhs_idx, epilogues):
        for c in range(seq // PROJ_ROWS):
            r0 = c * PROJ_ROWS
            acc = jnp.dot(lhs_ref[lhs_idx, pl.ds(r0, PROJ_ROWS), :], w_ref[...],
                          preferred_element_type=F32)
            for c0, width, fn in epilogues:
                o_ref[pl.ds(r0, PROJ_ROWS), c0:c0 + width] = (
                    fn(acc[:, c0:c0 + width], r0).astype(BF16))

    def gates(acc, r0):
        return _sigmoid(acc + bg_ref[...])

    def layer_norm_gelu(acc, r0):
        z = _gelu_tanh(acc)
        mu = jnp.mean(z, axis=-1, keepdims=True)
        zc = z - mu
        var = jnp.mean(zc * zc, axis=-1, keepdims=True)
        return zc * lax.rsqrt(var + NORM_EPS) * lng_ref[...] + lnb_ref[...]

    def rotary(scale):
        def fn(acc, r0):
            cos = rope_ref[0, pl.ds(r0, PROJ_ROWS), :]
            sin = rope_ref[1, pl.ds(r0, PROJ_ROWS), :]
            heads = []
            for h in range(HEADS_PER_GROUP):
                t = acc[:, h * HEAD_DIM:(h + 1) * HEAD_DIM]
                out = t * cos + pltpu.roll(t, HEAD_DIM // 2, 1) * sin
                heads.append(out if scale is None else out * scale)
            return jnp.concatenate(heads, axis=1)

        return fn

    @pl.when(j == TILE_A)
    def _tile_a():
        matmul_rows(0, [(0, 2 * PIECE, gates),
                        (2 * PIECE, PIECE, lambda acc, r0: _gelu_tanh(acc))])

    @pl.when(j == TILE_B)
    def _tile_b():
        matmul_rows(0, [(0, 2 * PIECE, gates), (2 * PIECE, PIECE, layer_norm_gelu)])

    @pl.when(j >= TILE_QKV)
    def _tile_qkv():
        matmul_rows(j - TILE_QKV, [(0, PIECE, rotary(QK_SCALE)),
                                   (PIECE, PIECE, rotary(None)),
                                   (2 * PIECE, PIECE, lambda acc, r0: acc)])


def _proj_call(x, gain, w_in, b_gate, ln_gain, ln_bias, rope_tab):
    bsz, seq, d_model = x.shape
    assert w_in.shape[1] == N_TILES * TN
    kern = functools.partial(_proj_kernel, seq=seq, d_model=d_model)
    return pl.pallas_call(
        kern,
        name="proj",
        grid=(bsz, N_TILES),
        in_specs=[
            pl.BlockSpec((None, seq, d_model), lambda b, j: (b, 0, 0)),
            pl.BlockSpec((1, d_model), lambda b, j: (0, 0)),
            pl.BlockSpec((d_model, TN), lambda b, j: (0, j)),
            pl.BlockSpec((1, 2 * PIECE), lambda b, j: (0, jnp.minimum(j, TILE_B))),
            pl.BlockSpec((1, PIECE), lambda b, j: (0, 0)),
            pl.BlockSpec((1, PIECE), lambda b, j: (0, 0)),
            pl.BlockSpec((None, 2, seq, HEAD_DIM),
                         lambda b, j: (jnp.maximum(j - TILE_QKV, 0), 0, 0, 0)),
        ],
        out_specs=pl.BlockSpec((None, None, seq, TN), lambda b, j: (j, b, 0, 0)),
        out_shape=jax.ShapeDtypeStruct((N_TILES, bsz, seq, TN), BF16),
        scratch_shapes=[
            pltpu.VMEM((N_GROUPS, seq, d_model), BF16),
            pltpu.VMEM((2, d_model // LANES, PROLOGUE_ROWS, LANES), F32),
        ],
        compiler_params=pltpu.CompilerParams(
            dimension_semantics=("arbitrary", "arbitrary"),
            vmem_limit_bytes=VMEM_LIMIT_BYTES),
    )(x, gain, w_in, b_gate, ln_gain, ln_bias, rope_tab)


def _attn_group(q_ref, k_ref, v_ref, acc_slab, den_slab, max_slab, bias_ref, s_scr, p_scr,
                gi, d, seq):
    sub_len = seq // d
    n_blk = sub_len // Q_BLOCK
    win = min(2 * Q_BLOCK, sub_len)
    ones = jnp.ones((win, HEAD_DIM), BF16)

    def body(step, carry):
        blocks = []
        for e in range(BLOCKS_PER_STEP):
            f = step * BLOCKS_PER_STEP + e
            slot = f // n_blk
            q_off = (f % n_blk) * Q_BLOCK
            k_off = jnp.clip(q_off - RADIUS, 0, sub_len - win)
            blocks.append(dict(
                slot=slot, q_off=q_off, band=(q_off - k_off) // RADIUS,
                q0=pl.multiple_of(slot * sub_len + q_off, Q_BLOCK),
                k0=pl.multiple_of(slot * sub_len + k_off, RADIUS)))
        units = [(e, h) for e in range(BLOCKS_PER_STEP) for h in range(HEADS_PER_GROUP)]

        for u, (e, h) in enumerate(units):
            blk = blocks[e]
            hs = slice(h * HEAD_DIM, (h + 1) * HEAD_DIM)
            s_scr[u, :, :win] = lax.dot_general(
                q_ref[pl.ds(blk["q0"], Q_BLOCK), hs], k_ref[pl.ds(blk["k0"], win), hs],
                (((1,), (1,)), ((), ())), preferred_element_type=F32)

        row_max = []
        for u, (e, h) in enumerate(units):
            s = s_scr[u, :, :win] + bias_ref[blocks[e]["band"], :, :win]
            m = jnp.max(s, axis=-1, keepdims=True)
            p_scr[u, :, :win] = jnp.exp2(s - m).astype(BF16)
            row_max.append(m)

        for u, (e, h) in enumerate(units):
            blk = blocks[e]
            hs = slice(h * HEAD_DIM, (h + 1) * HEAD_DIM)
            v_ext = jnp.concatenate([v_ref[pl.ds(blk["k0"], win), hs], ones], axis=1)
            oe = jnp.dot(p_scr[u, :, :win], v_ext, preferred_element_type=F32)
            if d == 1:
                rows = pl.ds(blk["q0"], Q_BLOCK)
            else:
                rows = pl.ds(blk["q_off"] * d + _slot_residue(blk["slot"], d), Q_BLOCK, stride=d)
            idx = gi * HEADS_PER_GROUP + h
            acc_slab[idx, rows, :] = oe[:, :HEAD_DIM]
            den_slab[idx, rows, :] = oe[:, HEAD_DIM:]
            max_slab[idx, rows, :] = jnp.broadcast_to(row_max[u], (Q_BLOCK, HEAD_DIM))
        return carry

    lax.fori_loop(0, d * n_blk // BLOCKS_PER_STEP, body, 0)


def _attn_kernel(q_ref, k_ref, v_ref, y_ref, acc_slab, den_slab, max_slab, bias_ref,
                 s_scr, p_scr, *, seq):
    g = pl.program_id(1)

    @pl.when((pl.program_id(0) == 0) & (g == 0))
    def _band_bias():
        row = lax.broadcasted_iota(jnp.int32, (Q_BLOCK, 2 * Q_BLOCK), 0)
        col = lax.broadcasted_iota(jnp.int32, (Q_BLOCK, 2 * Q_BLOCK), 1)
        for band in range(N_BANDS):
            valid = jnp.abs(row - col + band * RADIUS) <= RADIUS
            bias_ref[band] = jnp.where(valid, 0.0, MASK_VALUE).astype(F32)

    for gi, (_, d) in enumerate(ATTN_PATTERNS):
        @pl.when(g == gi)
        def _group(gi=gi, d=d):
            _attn_group(q_ref, k_ref, v_ref, acc_slab, den_slab, max_slab, bias_ref,
                        s_scr, p_scr, gi, d, seq)

    @pl.when(g == N_GROUPS - 1)
    def _combine():
        def body(c, carry):
            rows = pl.ds(pl.multiple_of(c * COMBINE_ROWS, COMBINE_ROWS), COMBINE_ROWS)
            for h in range(HEADS_PER_GROUP):
                idx = [gi * HEADS_PER_GROUP + h for gi in range(N_GROUPS)]
                ms = [max_slab[i, rows, :] for i in idx]
                m = functools.reduce(jnp.maximum, ms)
                es = [jnp.exp2(mg - m) for mg in ms]
                den = functools.reduce(jnp.add, [e * den_slab[i, rows, :] for e, i in zip(es, idx)])
                num = functools.reduce(jnp.add, [e * acc_slab[i, rows, :] for e, i in zip(es, idx)])
                y_ref[rows, h * HEAD_DIM:(h + 1) * HEAD_DIM] = (num / den).astype(BF16)
            return carry

        lax.fori_loop(0, seq // COMBINE_ROWS, body, 0)


def _attn_call(proj):
    _, bsz, seq, _ = proj.shape
    n_slabs = N_GROUPS * HEADS_PER_GROUP

    def spec(piece):
        return pl.BlockSpec((None, None, seq, PIECE), lambda b, g: (TILE_QKV + g, b, 0, piece))

    return pl.pallas_call(
        functools.partial(_attn_kernel, seq=seq),
        name="attn",
        grid=(bsz, N_GROUPS),
        in_specs=[spec(0), spec(1), spec(2)],
        out_specs=pl.BlockSpec((None, seq, GROUP_WIDTH), lambda b, g: (b, 0, 0)),
        out_shape=jax.ShapeDtypeStruct((bsz, seq, GROUP_WIDTH), BF16),
        scratch_shapes=[
            pltpu.VMEM((n_slabs, seq, LANES), F32),
            pltpu.VMEM((n_slabs, seq, LANES), F32),
            pltpu.VMEM((n_slabs, seq, LANES), F32),
            pltpu.VMEM((N_BANDS, Q_BLOCK, 2 * Q_BLOCK), F32),
            pltpu.VMEM((BLOCKS_PER_STEP * HEADS_PER_GROUP, Q_BLOCK, 2 * Q_BLOCK), F32),
            pltpu.VMEM((BLOCKS_PER_STEP * HEADS_PER_GROUP, Q_BLOCK, 2 * Q_BLOCK), BF16),
        ],
        compiler_params=pltpu.CompilerParams(
            dimension_semantics=("arbitrary", "arbitrary"),
            vmem_limit_bytes=VMEM_LIMIT_BYTES),
    )(proj, proj, proj)


def _mix_kernel(zu_ref, zv_ref, ga_ref, gb_ref, yb_ref, x_ref,
                ws_ref, bsp_ref, wa_ref, wb_ref, wo_ref, gain_ref, o_ref, ya_ref):
    group_dim = GMLP_WIDTH // GMLP_GROUPS
    for c in range(MIX_ROWS // CHUNK):
        rows = slice(c * CHUNK, (c + 1) * CHUNK)
        sv = jnp.concatenate(
            [jnp.dot(ws_ref[g], zv_ref[rows, g * group_dim:(g + 1) * group_dim],
                     preferred_element_type=F32) for g in range(GMLP_GROUPS)], axis=1)
        ya_ref[rows, :] = (zu_ref[rows, :].astype(F32) * (sv + bsp_ref[...])).astype(BF16)
    a = jnp.dot(ya_ref[...], wa_ref[...], preferred_element_type=F32)
    b = jnp.dot(yb_ref[...], wb_ref[...], preferred_element_type=F32)
    merged = ga_ref[...].astype(F32) * a + gb_ref[...].astype(F32) * b
    mix = jnp.dot(merged.astype(BF16), wo_ref[...], preferred_element_type=F32)
    ms = jnp.mean(mix * mix, axis=-1, keepdims=True)
    o_ref[...] = x_ref[...] + mix * lax.rsqrt(ms + NORM_EPS) * gain_ref[...]


def _mix_call(proj, y_b, x, w_spatial, b_sp, w_a, w_b, w_out, gain):
    bsz, seq, d_model = x.shape

    def piece(tile, first_piece, n_pieces):
        width = n_pieces * PIECE
        return pl.BlockSpec((None, None, MIX_ROWS, width),
                            lambda b, i: (tile, b, i, first_piece // n_pieces))

    def whole(a):
        return pl.BlockSpec(a.shape, lambda b, i: (0,) * a.ndim)

    return pl.pallas_call(
        _mix_kernel,
        name="mix",
        grid=(bsz, seq // MIX_ROWS),
        in_specs=[
            piece(TILE_A, 2, 1), piece(TILE_B, 2, 1), piece(TILE_A, 0, 2), piece(TILE_B, 0, 2),
            pl.BlockSpec((None, MIX_ROWS, GROUP_WIDTH), lambda b, i: (b, i, 0)),
            pl.BlockSpec((None, MIX_ROWS, d_model), lambda b, i: (b, i, 0)),
            whole(w_spatial), whole(b_sp), whole(w_a), whole(w_b), whole(w_out), whole(gain),
        ],
        out_specs=pl.BlockSpec((None, MIX_ROWS, d_model), lambda b, i: (b, i, 0)),
        out_shape=jax.ShapeDtypeStruct((bsz, seq, d_model), F32),
        scratch_shapes=[pltpu.VMEM((MIX_ROWS, GMLP_WIDTH), BF16)],
        compiler_params=pltpu.CompilerParams(
            dimension_semantics=("parallel", "parallel"),
            vmem_limit_bytes=VMEM_LIMIT_BYTES),
    )(proj, proj, proj, proj, y_b, x, w_spatial, b_sp, w_a, w_b, w_out, gain)


def _mlp_kernel(h_ref, gpre_ref, wu_ref, wd_ref, gpost_ref, o_ref, n_ref, acc_ref, *, d_ff):
    h = h_ref[...]
    ms = jnp.mean(h * h, axis=-1, keepdims=True)
    n_ref[...] = (h * lax.rsqrt(ms + NORM_EPS) * gpre_ref[...]).astype(BF16)
    for c in range(d_ff // MLP_FF_CHUNK):
        cols = slice(c * MLP_FF_CHUNK, (c + 1) * MLP_FF_CHUNK)
        up = jnp.dot(n_ref[...], wu_ref[:, cols], preferred_element_type=F32)
        hid = jnp.square(jnp.maximum(up, 0.0)).astype(BF16)
        part = jnp.dot(hid, wd_ref[cols, :], preferred_element_type=F32)
        if c == 0:
            acc_ref[...] = part
        else:
            acc_ref[...] += part
    out = acc_ref[...]
    ms2 = jnp.mean(out * out, axis=-1, keepdims=True)
    o_ref[...] = h_ref[...] + out * lax.rsqrt(ms2 + NORM_EPS) * gpost_ref[...]


def _mlp_call(h, gain_pre, w_up, w_down, gain_post):
    bsz, seq, d_model = h.shape
    d_ff = w_up.shape[1]
    rows = bsz * seq
    h2 = h.reshape(rows, d_model)

    def whole(a):
        return pl.BlockSpec(a.shape, lambda i: (0,) * a.ndim, pipeline_mode=pl.Buffered(1))

    out = pl.pallas_call(
        functools.partial(_mlp_kernel, d_ff=d_ff),
        name="mlp",
        grid=(rows // MLP_ROWS,),
        in_specs=[
            pl.BlockSpec((MLP_ROWS, d_model), lambda i: (i, 0)),
            whole(gain_pre), whole(w_up), whole(w_down), whole(gain_post),
        ],
        out_specs=pl.BlockSpec((MLP_ROWS, d_model), lambda i: (i, 0)),
        out_shape=jax.ShapeDtypeStruct((rows, d_model), F32),
        scratch_shapes=[
            pltpu.VMEM((MLP_ROWS, d_model), BF16),
            pltpu.VMEM((MLP_ROWS, d_model), F32),
        ],
        compiler_params=pltpu.CompilerParams(
            dimension_semantics=("parallel",),
            vmem_limit_bytes=VMEM_LIMIT_BYTES),
    )(h2, gain_pre, w_up, w_down, gain_post)
    return out.reshape(bsz, seq, d_model)


def _rope_lane_order():
    half = ROPE_DIM // 2
    fill = HEAD_DIM // 2 - half
    return (list(range(half)) + list(range(ROPE_DIM, ROPE_DIM + fill))
            + list(range(half, ROPE_DIM)) + list(range(ROPE_DIM + fill, HEAD_DIM)))


def _source_piece(i):
    n_attn = 3 * N_GROUPS
    first_gate = 2 + n_attn
    in_qkv = i >= TILE_QKV * PIECES_PER_TILE
    a, c = i // PIECES_PER_TILE, i % PIECES_PER_TILE
    head_tiles = jnp.where(c < 2, first_gate + 2 * a + c, a)
    g, section = a - TILE_QKV, c
    return jnp.where(in_qkv, 2 + N_GROUPS * section + g, head_tiles)


def _wprep_kernel(w_ref, sel_ref, o_ref):
    i = pl.program_id(0)
    is_qk = (i >= TILE_QKV * PIECES_PER_TILE) & (i % PIECES_PER_TILE < 2)

    @pl.when(is_qk)
    def _reorder():
        for h in range(HEADS_PER_GROUP):
            hs = slice(h * HEAD_DIM, (h + 1) * HEAD_DIM)
            o_ref[:, hs] = jnp.dot(w_ref[:, hs].astype(BF16), sel_ref[...],
                                   preferred_element_type=F32).astype(BF16)

    @pl.when(jnp.logical_not(is_qk))
    def _copy():
        o_ref[...] = w_ref[...].astype(BF16)


def _regroup_w_in(w_in):
    d_model, d_in = w_in.shape
    select = np.zeros((HEAD_DIM, HEAD_DIM), np.float32)
    select[_rope_lane_order(), np.arange(HEAD_DIM)] = 1.0
    return pl.pallas_call(
        _wprep_kernel,
        name="wprep",
        grid=(d_in // PIECE,),
        in_specs=[
            pl.BlockSpec((d_model, PIECE), lambda i: (0, _source_piece(i))),
            pl.BlockSpec((HEAD_DIM, HEAD_DIM), lambda i: (0, 0)),
        ],
        out_specs=pl.BlockSpec((d_model, PIECE), lambda i: (0, i)),
        out_shape=jax.ShapeDtypeStruct((d_model, d_in), BF16),
        compiler_params=pltpu.CompilerParams(
            dimension_semantics=("parallel",),
            vmem_limit_bytes=VMEM_LIMIT_BYTES),
    )(w_in, jnp.asarray(select, BF16))


def _rope_tables(seq):
    half = ROPE_DIM // 2
    fill = HEAD_DIM // 2 - half
    inv_freq = np.float32(ROPE_THETA) ** (-np.arange(0, ROPE_DIM, 2, dtype=np.float32) / ROPE_DIM)
    ang = np.arange(seq, dtype=np.float32)[:, None] * inv_freq[None, :].astype(np.float32)
    cos, sin = np.cos(ang), np.sin(ang)
    ones = np.ones((seq, fill), np.float32)
    cos_l = np.concatenate([cos, ones, cos, ones], axis=1)
    sin_l = np.concatenate([-sin, 0.0 * ones, sin, 0.0 * ones], axis=1)
    nat = np.stack([cos_l, sin_l])
    per_group = []
    for _, d in ATTN_PATTERNS:
        by_residue = nat.reshape(2, seq // d, d, HEAD_DIM).transpose(0, 2, 1, 3)
        residues = [_slot_residue(slot, d) for slot in range(d)]
        per_group.append(by_residue[:, residues].reshape(2, seq, HEAD_DIM))
    return jnp.asarray(np.stack(per_group), F32)


def _layer(h, norm_mix_pre, w_in, b_gate, ln_v_gain, ln_v_bias, w_spatial, b_spatial,
           w_branch_a, w_branch_b, w_out, norm_mix_post, norm_mlp_pre, w_up, w_down,
           norm_mlp_post):
    seq = h.shape[1]
    row = lambda v: v.reshape(1, -1).astype(F32)
    proj = _proj_call(h, row(norm_mix_pre), _regroup_w_in(w_in), row(b_gate), row(ln_v_gain),
                      row(ln_v_bias), _rope_tables(seq))
    y_b = _attn_call(proj)
    b_sp = jnp.repeat(b_spatial.T.astype(F32), GMLP_WIDTH // GMLP_GROUPS, axis=1)
    h = _mix_call(proj, y_b, h, w_spatial.astype(BF16), b_sp, w_branch_a.astype(BF16),
                  w_branch_b.astype(BF16), w_out.astype(BF16), row(norm_mix_post))
    return _mlp_call(h, row(norm_mlp_pre), w_up.astype(BF16), w_down.astype(BF16),
                     row(norm_mlp_post))


def kernel(x, norm_mix_pre, w_in, b_gate, ln_v_gain, ln_v_bias, w_spatial, b_spatial,
           w_branch_a, w_branch_b, w_out, norm_mix_post, norm_mlp_pre, w_up, w_down,
           norm_mlp_post):
    h = x
    for l in range(w_in.shape[0]):
        h = _layer(h, norm_mix_pre[l], w_in[l], b_gate[l], ln_v_gain[l], ln_v_bias[l],
                   w_spatial[l], b_spatial[l], w_branch_a[l], w_branch_b[l], w_out[l],
                   norm_mix_post[l], norm_mlp_pre[l], w_up[l], w_down[l], norm_mlp_post[l])
    return h
```

```python
import functools
import math

import jax
import jax.numpy as jnp
import numpy as np
from jax import lax
from jax.experimental import pallas as pl
from jax.experimental.pallas import tpu as pltpu

GMLP_WIDTH = 512
GMLP_GROUPS = 4
CHUNK = 128
ATTN_PATTERNS = ((128, 1), (512, 4), (2048, 16))
N_GROUPS = len(ATTN_PATTERNS)
HEADS_PER_GROUP = 4
HEAD_DIM = 128
GROUP_WIDTH = HEADS_PER_GROUP * HEAD_DIM
ROPE_DIM = HEAD_DIM // 4
ROPE_THETA = 500000.0
NORM_EPS = 1e-6
MASK_VALUE = -1e30
LOG2E = math.log2(math.e)
QK_SCALE = LOG2E / math.sqrt(HEAD_DIM)
RADIUS = 64
assert all(w // (2 * d) == RADIUS for w, d in ATTN_PATTERNS)
DEINT_STRIDE = 4
assert tuple(d for _, d in ATTN_PATTERNS) == (1, DEINT_STRIDE, DEINT_STRIDE ** 2)


def _slot_residue(slot, d):
    if d == DEINT_STRIDE ** 2:
        return DEINT_STRIDE * (slot % DEINT_STRIDE) + slot // DEINT_STRIDE
    return slot


LANES = 128
VMEM_LIMIT_BYTES = 56 * 1024 * 1024

PIECE = 512
PIECES_PER_TILE = 3
TN = PIECES_PER_TILE * PIECE
PROLOGUE_ROWS = 256
PROJ_ROWS = 256
Q_BLOCK = 128
BLOCKS_PER_STEP = 4
N_BANDS = 3
GROUP_ORDER = (1, 2, 0)
MIX_ROWS = 1024
MLP_ROWS = 1024
MLP_FF_CHUNK = 1024

BF16 = jnp.bfloat16
F32 = jnp.float32

TILE_A, TILE_B, TILE_QKV = 0, 1, 2
N_TILES = TILE_QKV + N_GROUPS


def _gelu_tanh(x):
    c = math.sqrt(2.0 / math.pi)
    return x * (0.5 * (1.0 + jnp.tanh(c * (x + 0.044715 * (x * x * x)))))


def _sigmoid(x):
    return 1.0 / (1.0 + jnp.exp(-x))


def _proj_kernel(x_ref, gain_ref, w_ref, bg_ref, lng_ref, lnb_ref, rope_ref, o_ref,
                 lhs_ref, slab_ref, *, seq, d_model):
    j = pl.program_id(1)
    n_slabs = d_model // LANES

    @pl.when(j == 0)
    def _prologue():
        st = DEINT_STRIDE
        cnt1 = PROLOGUE_ROWS // st
        cnt2 = cnt1 // st

        def body(c, carry):
            r0 = pl.multiple_of(c * PROLOGUE_ROWS, PROLOGUE_ROWS)
            xc = x_ref[pl.ds(r0, PROLOGUE_ROWS), :]
            ms = jnp.mean(xc * xc, axis=-1, keepdims=True)
            n = xc * lax.rsqrt(ms + NORM_EPS) * gain_ref[...]
            lhs_ref[0, pl.ds(r0, PROLOGUE_ROWS), :] = n.astype(BF16)
            for s in range(n_slabs):
                lanes = slice(s * LANES, (s + 1) * LANES)
                slab_ref[0, s] = n[:, lanes]
                for r1 in range(st):
                    piece = slab_ref[0, s, pl.ds(r1, cnt1, stride=st), :]
                    dst1 = pl.multiple_of(r1 * (seq // st) + c * cnt1, cnt1)
                    lhs_ref[1, pl.ds(dst1, cnt1), lanes] = piece.astype(BF16)
                    slab_ref[1, s, r1 * cnt1:(r1 + 1) * cnt1, :] = piece
                for r1 in range(st):
                    for r2 in range(st):
                        sub = slab_ref[1, s, pl.ds(r1 * cnt1 + r2, cnt2, stride=st), :]
                        slot = r1 * st + r2
                        dst2 = pl.multiple_of(slot * (seq // (st * st)) + c * cnt2, cnt2)
                        lhs_ref[2, pl.ds(dst2, cnt2), lanes] = sub.astype(BF16)
            return carry

        lax.fori_loop(0, seq // PROLOGUE_ROWS, body, 0)

    def matmul_rows(lhs_idx, epilogues):
        for c in range(seq // PROJ_ROWS):
            r0 = c * PROJ_ROWS
            acc = jnp.dot(lhs_ref[lhs_idx, pl.ds(r0, PROJ_ROWS), :], w_ref[...],
                          preferred_element_type=F32)
            for c0, width, fn in epilogues:
                o_ref[pl.ds(r0, PROJ_ROWS), c0:c0 + width] = (
                    fn(acc[:, c0:c0 + width], r0).astype(BF16))

    def gates(acc, r0):
        return _sigmoid(acc + bg_ref[...])

    def layer_norm_gelu(acc, r0):
        z = _gelu_tanh(acc)
        mu = jnp.mean(z, axis=-1, keepdims=True)
        zc = z - mu
        var = jnp.mean(zc * zc, axis=-1, keepdims=True)
        return zc * lax.rsqrt(var + NORM_EPS) * lng_ref[...] + lnb_ref[...]

    def rotary(scale):
        def fn(acc, r0):
            cos = rope_ref[0, pl.ds(r0, PROJ_ROWS), :]
            sin = rope_ref[1, pl.ds(r0, PROJ_ROWS), :]
            heads = []
            for h in range(HEADS_PER_GROUP):
                t = acc[:, h * HEAD_DIM:(h + 1) * HEAD_DIM]
                out = t * cos + pltpu.roll(t, HEAD_DIM // 2, 1) * sin
                heads.append(out if scale is None else out * scale)
            return jnp.concatenate(heads, axis=1)

        return fn

    @pl.when(j == TILE_A)
    def _tile_a():
        matmul_rows(0, [(0, 2 * PIECE, gates),
                        (2 * PIECE, PIECE, lambda acc, r0: _gelu_tanh(acc))])

    @pl.when(j == TILE_B)
    def _tile_b():
        matmul_rows(0, [(0, 2 * PIECE, gates), (2 * PIECE, PIECE, layer_norm_gelu)])

    @pl.when(j >= TILE_QKV)
    def _tile_qkv():
        matmul_rows(j - TILE_QKV, [(0, PIECE, rotary(QK_SCALE)),
                                   (PIECE, PIECE, rotary(None)),
                                   (2 * PIECE, PIECE, lambda acc, r0: acc)])


def _proj_call(x, gain, w_in, b_gate, ln_gain, ln_bias, rope_tab):
    bsz, seq, d_model = x.shape
    assert w_in.shape[1] == N_TILES * TN
    kern = functools.partial(_proj_kernel, seq=seq, d_model=d_model)
    return pl.pallas_call(
        kern,
        name="proj",
        grid=(bsz, N_TILES),
        in_specs=[
            pl.BlockSpec((None, seq, d_model), lambda b, j: (b, 0, 0)),
            pl.BlockSpec((1, d_model), lambda b, j: (0, 0)),
            pl.BlockSpec((d_model, TN), lambda b, j: (0, j)),
            pl.BlockSpec((1, 2 * PIECE), lambda b, j: (0, jnp.minimum(j, TILE_B))),
            pl.BlockSpec((1, PIECE), lambda b, j: (0, 0)),
            pl.BlockSpec((1, PIECE), lambda b, j: (0, 0)),
            pl.BlockSpec((None, 2, seq, HEAD_DIM),
                         lambda b, j: (jnp.maximum(j - TILE_QKV, 0), 0, 0, 0)),
        ],
        out_specs=pl.BlockSpec((None, None, seq, TN), lambda b, j: (j, b, 0, 0)),
        out_shape=jax.ShapeDtypeStruct((N_TILES, bsz, seq, TN), BF16),
        scratch_shapes=[
            pltpu.VMEM((N_GROUPS, seq, d_model), BF16),
            pltpu.VMEM((2, d_model // LANES, PROLOGUE_ROWS, LANES), F32),
        ],
        compiler_params=pltpu.CompilerParams(
            dimension_semantics=("arbitrary", "arbitrary"),
            vmem_limit_bytes=VMEM_LIMIT_BYTES),
    )(x, gain, w_in, b_gate, ln_gain, ln_bias, rope_tab)


def _merge(old, new):
    acc0, den0, max0 = old
    acc1, den1, max1 = new
    m = jnp.maximum(max0, max1)
    w0 = jnp.exp2(max0 - m)
    w1 = jnp.exp2(max1 - m)
    return acc0 * w0 + acc1 * w1, den0 * w0 + den1 * w1, m


def _attn_group(q_ref, k_ref, v_ref, y_ref, acc_slab, den_slab, max_slab, tmp_ref,
                bias_ref, s_scr, p_scr, d, seq):
    st = DEINT_STRIDE
    sub_len = seq // d
    n_blk = sub_len // Q_BLOCK
    win = min(2 * Q_BLOCK, sub_len)
    ones = jnp.ones((win, HEAD_DIM), BF16)
    slabs = (acc_slab, den_slab, max_slab)
    bps = BLOCKS_PER_STEP

    def body(step, carry):
        blocks = []
        for e in range(bps):
            f = step * bps + e
            slot = f // n_blk
            q_off = (f % n_blk) * Q_BLOCK
            k_off = jnp.clip(q_off - RADIUS, 0, sub_len - win)
            blocks.append(dict(
                slot=slot, q_off=q_off, band=(q_off - k_off) // RADIUS,
                q0=pl.multiple_of(slot * sub_len + q_off, Q_BLOCK),
                k0=pl.multiple_of(slot * sub_len + k_off, RADIUS)))
        units = [(e, h) for e in range(bps) for h in range(HEADS_PER_GROUP)]

        for u, (e, h) in enumerate(units):
            blk = blocks[e]
            hs = slice(h * HEAD_DIM, (h + 1) * HEAD_DIM)
            s_scr[u, :, :win] = lax.dot_general(
                q_ref[pl.ds(blk["q0"], Q_BLOCK), hs], k_ref[pl.ds(blk["k0"], win), hs],
                (((1,), (1,)), ((), ())), preferred_element_type=F32)

        row_max = []
        for u, (e, h) in enumerate(units):
            s = s_scr[u, :, :win] + bias_ref[blocks[e]["band"], :, :win]
            m = jnp.max(s, axis=-1, keepdims=True)
            p_scr[u, :, :win] = jnp.exp2(s - m).astype(BF16)
            row_max.append(m)

        for u, (e, h) in enumerate(units):
            blk = blocks[e]
            hs = slice(h * HEAD_DIM, (h + 1) * HEAD_DIM)
            v_ext = jnp.concatenate([v_ref[pl.ds(blk["k0"], win), hs], ones], axis=1)
            oe = jnp.dot(p_scr[u, :, :win], v_ext, preferred_element_type=F32)
            new = (oe[:, :HEAD_DIM], oe[:, HEAD_DIM:],
                   jnp.broadcast_to(row_max[u], (Q_BLOCK, HEAD_DIM)))
            if d == st:
                rows = pl.ds(blk["q0"], Q_BLOCK)
                for slab, val in zip(slabs, new):
                    slab[h, rows, :] = val
            elif d == st * st:
                r1, r2 = blk["slot"] // st, blk["slot"] % st
                rows = pl.ds(r1 * (seq // st) + blk["q_off"] * st + r2, Q_BLOCK, stride=st)
                merged = _merge([slab[h, rows, :] for slab in slabs], new)
                for slab, val in zip(slabs, merged):
                    slab[h, rows, :] = val
            else:
                cnt = Q_BLOCK // st
                old = []
                for k, slab in enumerate(slabs):
                    t = len(slabs) * u + k
                    for r in range(st):
                        src = pl.multiple_of(r * (seq // st) + blk["q0"] // st, cnt)
                        tmp_ref[t, pl.ds(r, cnt, stride=st), :] = slab[h, pl.ds(src, cnt), :]
                    old.append(tmp_ref[t])
                acc, den, _ = _merge(old, new)
                y_ref[pl.ds(blk["q0"], Q_BLOCK), hs] = (acc / den).astype(BF16)
        return carry

    lax.fori_loop(0, d * n_blk // bps, body, 0)


def _attn_kernel(q_ref, k_ref, v_ref, y_ref, acc_slab, den_slab, max_slab, tmp_ref,
                 bias_ref, s_scr, p_scr, *, seq):
    g = pl.program_id(1)

    @pl.when((pl.program_id(0) == 0) & (g == 0))
    def _band_bias():
        row = lax.broadcasted_iota(jnp.int32, (Q_BLOCK, 2 * Q_BLOCK), 0)
        col = lax.broadcasted_iota(jnp.int32, (Q_BLOCK, 2 * Q_BLOCK), 1)
        for band in range(N_BANDS):
            valid = jnp.abs(row - col + band * RADIUS) <= RADIUS
            bias_ref[band] = jnp.where(valid, 0.0, MASK_VALUE).astype(F32)

    for step, gi in enumerate(GROUP_ORDER):
        @pl.when(g == step)
        def _group(d=ATTN_PATTERNS[gi][1]):
            _attn_group(q_ref, k_ref, v_ref, y_ref, acc_slab, den_slab, max_slab, tmp_ref,
                        bias_ref, s_scr, p_scr, d, seq)


def _attn_call(proj):
    _, bsz, seq, _ = proj.shape
    n_slabs = HEADS_PER_GROUP
    n_units = BLOCKS_PER_STEP * HEADS_PER_GROUP
    assert all(GROUP_ORDER[s] == (s + GROUP_ORDER[0]) % N_GROUPS for s in range(N_GROUPS))

    def spec(piece):
        return pl.BlockSpec(
            (None, None, seq, PIECE),
            lambda b, g: (TILE_QKV + (g + GROUP_ORDER[0]) % N_GROUPS, b, 0, piece))

    return pl.pallas_call(
        functools.partial(_attn_kernel, seq=seq),
        name="attn",
        grid=(bsz, N_GROUPS),
        in_specs=[spec(0), spec(1), spec(2)],
        out_specs=pl.BlockSpec((None, seq, GROUP_WIDTH), lambda b, g: (b, 0, 0)),
        out_shape=jax.ShapeDtypeStruct((bsz, seq, GROUP_WIDTH), BF16),
        scratch_shapes=[
            pltpu.VMEM((n_slabs, seq, LANES), F32),
            pltpu.VMEM((n_slabs, seq, LANES), F32),
            pltpu.VMEM((n_slabs, seq, LANES), F32),
            pltpu.VMEM((3 * n_units, Q_BLOCK, LANES), F32),
            pltpu.VMEM((N_BANDS, Q_BLOCK, 2 * Q_BLOCK), F32),
            pltpu.VMEM((n_units, Q_BLOCK, 2 * Q_BLOCK), F32),
            pltpu.VMEM((n_units, Q_BLOCK, 2 * Q_BLOCK), BF16),
        ],
        compiler_params=pltpu.CompilerParams(
            dimension_semantics=("arbitrary", "arbitrary"),
            vmem_limit_bytes=VMEM_LIMIT_BYTES),
    )(proj, proj, proj)


def _mix_kernel(zu_ref, zv_ref, ga_ref, gb_ref, yb_ref, x_ref,
                ws_ref, bsp_ref, wa_ref, wb_ref, wo_ref, gain_ref, o_ref, ya_ref):
    group_dim = GMLP_WIDTH // GMLP_GROUPS
    for c in range(MIX_ROWS // CHUNK):
        rows = slice(c * CHUNK, (c + 1) * CHUNK)
        sv = jnp.concatenate(
            [jnp.dot(ws_ref[g], zv_ref[rows, g * group_dim:(g + 1) * group_dim],
                     preferred_element_type=F32) for g in range(GMLP_GROUPS)], axis=1)
        ya_ref[rows, :] = (zu_ref[rows, :].astype(F32) * (sv + bsp_ref[...])).astype(BF16)
    a = jnp.dot(ya_ref[...], wa_ref[...], preferred_element_type=F32)
    b = jnp.dot(yb_ref[...], wb_ref[...], preferred_element_type=F32)
    merged = ga_ref[...].astype(F32) * a + gb_ref[...].astype(F32) * b
    mix = jnp.dot(merged.astype(BF16), wo_ref[...], preferred_element_type=F32)
    ms = jnp.mean(mix * mix, axis=-1, keepdims=True)
    o_ref[...] = x_ref[...] + mix * lax.rsqrt(ms + NORM_EPS) * gain_ref[...]


def _mix_call(proj, y_b, x, w_spatial, b_sp, w_a, w_b, w_out, gain):
    bsz, seq, d_model = x.shape

    def piece(tile, first_piece, n_pieces):
        width = n_pieces * PIECE
        return pl.BlockSpec((None, None, MIX_ROWS, width),
                            lambda b, i: (tile, b, i, first_piece // n_pieces))

    def whole(a):
        return pl.BlockSpec(a.shape, lambda b, i: (0,) * a.ndim)

    return pl.pallas_call(
        _mix_kernel,
        name="mix",
        grid=(bsz, seq // MIX_ROWS),
        in_specs=[
            piece(TILE_A, 2, 1), piece(TILE_B, 2, 1), piece(TILE_A, 0, 2), piece(TILE_B, 0, 2),
            pl.BlockSpec((None, MIX_ROWS, GROUP_WIDTH), lambda b, i: (b, i, 0)),
            pl.BlockSpec((None, MIX_ROWS, d_model), lambda b, i: (b, i, 0)),
            whole(w_spatial), whole(b_sp), whole(w_a), whole(w_b), whole(w_out), whole(gain),
        ],
        out_specs=pl.BlockSpec((None, MIX_ROWS, d_model), lambda b, i: (b, i, 0)),
        out_shape=jax.ShapeDtypeStruct((bsz, seq, d_model), F32),
        scratch_shapes=[pltpu.VMEM((MIX_ROWS, GMLP_WIDTH), BF16)],
        compiler_params=pltpu.CompilerParams(
            dimension_semantics=("parallel", "parallel"),
            vmem_limit_bytes=VMEM_LIMIT_BYTES),
    )(proj, proj, proj, proj, y_b, x, w_spatial, b_sp, w_a, w_b, w_out, gain)


def _mlp_kernel(h_ref, gpre_ref, wu_ref, wd_ref, gpost_ref, o_ref, n_ref, acc_ref, *, d_ff):
    h = h_ref[...]
    ms = jnp.mean(h * h, axis=-1, keepdims=True)
    n_ref[...] = (h * lax.rsqrt(ms + NORM_EPS) * gpre_ref[...]).astype(BF16)
    for c in range(d_ff // MLP_FF_CHUNK):
        cols = slice(c * MLP_FF_CHUNK, (c + 1) * MLP_FF_CHUNK)
        up = jnp.dot(n_ref[...], wu_ref[:, cols], preferred_element_type=F32)
        hid = jnp.square(jnp.maximum(up, 0.0)).astype(BF16)
        part = jnp.dot(hid, wd_ref[cols, :], preferred_element_type=F32)
        if c == 0:
            acc_ref[...] = part
        else:
            acc_ref[...] += part
    out = acc_ref[...]
    ms2 = jnp.mean(out * out, axis=-1, keepdims=True)
    o_ref[...] = h_ref[...] + out * lax.rsqrt(ms2 + NORM_EPS) * gpost_ref[...]


def _mlp_call(h, gain_pre, w_up, w_down, gain_post):
    bsz, seq, d_model = h.shape
    d_ff = w_up.shape[1]
    rows = bsz * seq
    h2 = h.reshape(rows, d_model)

    def whole(a):
        return pl.BlockSpec(a.shape, lambda i: (0,) * a.ndim, pipeline_mode=pl.Buffered(1))

    out = pl.pallas_call(
        functools.partial(_mlp_kernel, d_ff=d_ff),
        name="mlp",
        grid=(rows // MLP_ROWS,),
        in_specs=[
            pl.BlockSpec((MLP_ROWS, d_model), lambda i: (i, 0)),
            whole(gain_pre), whole(w_up), whole(w_down), whole(gain_post),
        ],
        out_specs=pl.BlockSpec((MLP_ROWS, d_model), lambda i: (i, 0)),
        out_shape=jax.ShapeDtypeStruct((rows, d_model), F32),
        scratch_shapes=[
            pltpu.VMEM((MLP_ROWS, d_model), BF16),
            pltpu.VMEM((MLP_ROWS, d_model), F32),
        ],
        compiler_params=pltpu.CompilerParams(
            dimension_semantics=("parallel",),
            vmem_limit_bytes=VMEM_LIMIT_BYTES),
    )(h2, gain_pre, w_up, w_down, gain_post)
    return out.reshape(bsz, seq, d_model)


def _rope_lane_order():
    half = ROPE_DIM // 2
    fill = HEAD_DIM // 2 - half
    return (list(range(half)) + list(range(ROPE_DIM, ROPE_DIM + fill))
            + list(range(half, ROPE_DIM)) + list(range(ROPE_DIM + fill, HEAD_DIM)))


def _source_piece(i):
    n_attn = 3 * N_GROUPS
    first_gate = 2 + n_attn
    in_qkv = i >= TILE_QKV * PIECES_PER_TILE
    a, c = i // PIECES_PER_TILE, i % PIECES_PER_TILE
    head_tiles = jnp.where(c < 2, first_gate + 2 * a + c, a)
    g, section = a - TILE_QKV, c
    return jnp.where(in_qkv, 2 + N_GROUPS * section + g, head_tiles)


def _wprep_kernel(w_ref, sel_ref, o_ref):
    i = pl.program_id(0)
    is_qk = (i >= TILE_QKV * PIECES_PER_TILE) & (i % PIECES_PER_TILE < 2)

    @pl.when(is_qk)
    def _reorder():
        for h in range(HEADS_PER_GROUP):
            hs = slice(h * HEAD_DIM, (h + 1) * HEAD_DIM)
            o_ref[:, hs] = jnp.dot(w_ref[:, hs].astype(BF16), sel_ref[...],
                                   preferred_element_type=F32).astype(BF16)

    @pl.when(jnp.logical_not(is_qk))
    def _copy():
        o_ref[...] = w_ref[...].astype(BF16)


def _regroup_w_in(w_in):
    d_model, d_in = w_in.shape
    select = np.zeros((HEAD_DIM, HEAD_DIM), np.float32)
    select[_rope_lane_order(), np.arange(HEAD_DIM)] = 1.0
    return pl.pallas_call(
        _wprep_kernel,
        name="wprep",
        grid=(d_in // PIECE,),
        in_specs=[
            pl.BlockSpec((d_model, PIECE), lambda i: (0, _source_piece(i))),
            pl.BlockSpec((HEAD_DIM, HEAD_DIM), lambda i: (0, 0)),
        ],
        out_specs=pl.BlockSpec((d_model, PIECE), lambda i: (0, i)),
        out_shape=jax.ShapeDtypeStruct((d_model, d_in), BF16),
        compiler_params=pltpu.CompilerParams(
            dimension_semantics=("parallel",),
            vmem_limit_bytes=VMEM_LIMIT_BYTES),
    )(w_in, jnp.asarray(select, BF16))


def _rope_tables(seq):
    half = ROPE_DIM // 2
    fill = HEAD_DIM // 2 - half
    inv_freq = np.float32(ROPE_THETA) ** (-np.arange(0, ROPE_DIM, 2, dtype=np.float32) / ROPE_DIM)
    ang = np.arange(seq, dtype=np.float32)[:, None] * inv_freq[None, :].astype(np.float32)
    cos, sin = np.cos(ang), np.sin(ang)
    ones = np.ones((seq, fill), np.float32)
    cos_l = np.concatenate([cos, ones, cos, ones], axis=1)
    sin_l = np.concatenate([-sin, 0.0 * ones, sin, 0.0 * ones], axis=1)
    nat = np.stack([cos_l, sin_l])
    per_group = []
    for _, d in ATTN_PATTERNS:
        by_residue = nat.reshape(2, seq // d, d, HEAD_DIM).transpose(0, 2, 1, 3)
        residues = [_slot_residue(slot, d) for slot in range(d)]
        per_group.append(by_residue[:, residues].reshape(2, seq, HEAD_DIM))
    return jnp.asarray(np.stack(per_group), F32)


def _layer(h, norm_mix_pre, w_in, b_gate, ln_v_gain, ln_v_bias, w_spatial, b_spatial,
           w_branch_a, w_branch_b, w_out, norm_mix_post, norm_mlp_pre, w_up, w_down,
           norm_mlp_post):
    seq = h.shape[1]
    row = lambda v: v.reshape(1, -1).astype(F32)
    proj = _proj_call(h, row(norm_mix_pre), _regroup_w_in(w_in), row(b_gate), row(ln_v_gain),
                      row(ln_v_bias), _rope_tables(seq))
    y_b = _attn_call(proj)
    b_sp = jnp.repeat(b_spatial.T.astype(F32), GMLP_WIDTH // GMLP_GROUPS, axis=1)
    h = _mix_call(proj, y_b, h, w_spatial.astype(BF16), b_sp, w_branch_a.astype(BF16),
                  w_branch_b.astype(BF16), w_out.astype(BF16), row(norm_mix_post))
    return _mlp_call(h, row(norm_mlp_pre), w_up.astype(BF16), w_down.astype(BF16),
                     row(norm_mlp_post))


def kernel(x, norm_mix_pre, w_in, b_gate, ln_v_gain, ln_v_bias, w_spatial, b_spatial,
           w_branch_a, w_branch_b, w_out, norm_mix_post, norm_mlp_pre, w_up, w_down,
           norm_mlp_post):
    h = x
    for l in range(w_in.shape[0]):
        h = _layer(h, norm_mix_pre[l], w_in[l], b_gate[l], ln_v_gain[l], ln_v_bias[l],
                   w_spatial[l], b_spatial[l], w_branch_a[l], w_branch_b[l], w_out[l],
                   norm_mix_post[l], norm_mlp_pre[l], w_up[l], w_down[l], norm_mlp_post[l])
    return h
```

```python
import functools
import math

import jax
import jax.numpy as jnp
import numpy as np
from jax import lax
from jax.experimental import pallas as pl
from jax.experimental.pallas import tpu as pltpu

GMLP_WIDTH = 512
GMLP_GROUPS = 4
CHUNK = 128
ATTN_PATTERNS = ((128, 1), (512, 4), (2048, 16))
N_GROUPS = len(ATTN_PATTERNS)
HEADS_PER_GROUP = 4
HEAD_DIM = 128
GROUP_WIDTH = HEADS_PER_GROUP * HEAD_DIM
ROPE_DIM = HEAD_DIM // 4
ROPE_THETA = 500000.0
NORM_EPS = 1e-6
MASK_VALUE = -1e30
LOG2E = math.log2(math.e)
QK_SCALE = LOG2E / math.sqrt(HEAD_DIM)
RADIUS = 64
assert all(w // (2 * d) == RADIUS for w, d in ATTN_PATTERNS)
DEINT_STRIDE = 4
assert tuple(d for _, d in ATTN_PATTERNS) == (1, DEINT_STRIDE, DEINT_STRIDE ** 2)


def _slot_residue(slot, d):
    if d == DEINT_STRIDE ** 2:
        return DEINT_STRIDE * (slot % DEINT_STRIDE) + slot // DEINT_STRIDE
    return slot


LANES = 128
VMEM_LIMIT_BYTES = 56 * 1024 * 1024

PIECE = 512
PIECES_PER_TILE = 3
TN = PIECES_PER_TILE * PIECE
PROLOGUE_ROWS = 256
PROJ_ROWS = 256
Q_BLOCK = 128
BLOCKS_PER_STEP = 8
N_BANDS = 3
GROUP_ORDER = (1, 2, 0)
MIX_ROWS = 1024
MLP_ROWS = 1024
MLP_FF_CHUNK = 1024

BF16 = jnp.bfloat16
F32 = jnp.float32

TILE_A, TILE_B, TILE_QKV = 0, 1, 2
N_TILES = TILE_QKV + N_GROUPS


def _gelu_tanh(x):
    c = math.sqrt(2.0 / math.pi)
    return x * (0.5 * (1.0 + jnp.tanh(c * (x + 0.044715 * (x * x * x)))))


def _sigmoid(x):
    return 1.0 / (1.0 + jnp.exp(-x))


def _proj_kernel(x_ref, gain_ref, w_ref, bg_ref, lng_ref, lnb_ref, rope_ref, o_ref,
                 lhs_ref, slab_ref, *, seq, d_model):
    j = pl.program_id(1)
    n_slabs = d_model // LANES

    @pl.when(j == 0)
    def _prologue():
        st = DEINT_STRIDE
        cnt1 = PROLOGUE_ROWS // st
        cnt2 = cnt1 // st

        def body(c, carry):
            r0 = pl.multiple_of(c * PROLOGUE_ROWS, PROLOGUE_ROWS)
            xc = x_ref[pl.ds(r0, PROLOGUE_ROWS), :]
            ms = jnp.mean(xc * xc, axis=-1, keepdims=True)
            n = xc * lax.rsqrt(ms + NORM_EPS) * gain_ref[...]
            lhs_ref[0, pl.ds(r0, PROLOGUE_ROWS), :] = n.astype(BF16)
            for s in range(n_slabs):
                lanes = slice(s * LANES, (s + 1) * LANES)
                slab_ref[0, s] = n[:, lanes]
                for r1 in range(st):
                    piece = slab_ref[0, s, pl.ds(r1, cnt1, stride=st), :]
                    dst1 = pl.multiple_of(r1 * (seq // st) + c * cnt1, cnt1)
                    lhs_ref[1, pl.ds(dst1, cnt1), lanes] = piece.astype(BF16)
                    slab_ref[1, s, r1 * cnt1:(r1 + 1) * cnt1, :] = piece
                for r1 in range(st):
                    for r2 in range(st):
                        sub = slab_ref[1, s, pl.ds(r1 * cnt1 + r2, cnt2, stride=st), :]
                        slot = r1 * st + r2
                        dst2 = pl.multiple_of(slot * (seq // (st * st)) + c * cnt2, cnt2)
                        lhs_ref[2, pl.ds(dst2, cnt2), lanes] = sub.astype(BF16)
            return carry

        lax.fori_loop(0, seq // PROLOGUE_ROWS, body, 0)

    def matmul_rows(lhs_idx, epilogues):
        for c in range(seq // PROJ_ROWS):
            r0 = c * PROJ_ROWS
            acc = jnp.dot(lhs_ref[lhs_idx, pl.ds(r0, PROJ_ROWS), :], w_ref[...],
                          preferred_element_type=F32)
            for c0, width, fn in epilogues:
                o_ref[pl.ds(r0, PROJ_ROWS), c0:c0 + width] = (
                    fn(acc[:, c0:c0 + width], r0).astype(BF16))

    def gates(acc, r0):
        return _sigmoid(acc + bg_ref[...])

    def layer_norm_gelu(acc, r0):
        z = _gelu_tanh(acc)
        mu = jnp.mean(z, axis=-1, keepdims=True)
        zc = z - mu
        var = jnp.mean(zc * zc, axis=-1, keepdims=True)
        return zc * lax.rsqrt(var + NORM_EPS) * lng_ref[...] + lnb_ref[...]

    def rotary(scale):
        def fn(acc, r0):
            cos = rope_ref[0, pl.ds(r0, PROJ_ROWS), :]
            sin = rope_ref[1, pl.ds(r0, PROJ_ROWS), :]
            heads = []
            for h in range(HEADS_PER_GROUP):
                t = acc[:, h * HEAD_DIM:(h + 1) * HEAD_DIM]
                out = t * cos + pltpu.roll(t, HEAD_DIM // 2, 1) * sin
                heads.append(out if scale is None else out * scale)
            return jnp.concatenate(heads, axis=1)

        return fn

    @pl.when(j == TILE_A)
    def _tile_a():
        matmul_rows(0, [(0, 2 * PIECE, gates),
                        (2 * PIECE, PIECE, lambda acc, r0: _gelu_tanh(acc))])

    @pl.when(j == TILE_B)
    def _tile_b():
        matmul_rows(0, [(0, 2 * PIECE, gates), (2 * PIECE, PIECE, layer_norm_gelu)])

    @pl.when(j >= TILE_QKV)
    def _tile_qkv():
        matmul_rows(j - TILE_QKV, [(0, PIECE, rotary(QK_SCALE)),
                                   (PIECE, PIECE, rotary(None)),
                                   (2 * PIECE, PIECE, lambda acc, r0: acc)])


def _proj_call(x, gain, w_in, b_gate, ln_gain, ln_bias, rope_tab):
    bsz, seq, d_model = x.shape
    assert w_in.shape[1] == N_TILES * TN
    kern = functools.partial(_proj_kernel, seq=seq, d_model=d_model)
    return pl.pallas_call(
        kern,
        name="proj",
        grid=(bsz, N_TILES),
        in_specs=[
            pl.BlockSpec((None, seq, d_model), lambda b, j: (b, 0, 0)),
            pl.BlockSpec((1, d_model), lambda b, j: (0, 0)),
            pl.BlockSpec((d_model, TN), lambda b, j: (0, j)),
            pl.BlockSpec((1, 2 * PIECE), lambda b, j: (0, jnp.minimum(j, TILE_B))),
            pl.BlockSpec((1, PIECE), lambda b, j: (0, 0)),
            pl.BlockSpec((1, PIECE), lambda b, j: (0, 0)),
            pl.BlockSpec((None, 2, seq, HEAD_DIM),
                         lambda b, j: (jnp.maximum(j - TILE_QKV, 0), 0, 0, 0)),
        ],
        out_specs=pl.BlockSpec((None, None, seq, TN), lambda b, j: (j, b, 0, 0)),
        out_shape=jax.ShapeDtypeStruct((N_TILES, bsz, seq, TN), BF16),
        scratch_shapes=[
            pltpu.VMEM((N_GROUPS, seq, d_model), BF16),
            pltpu.VMEM((2, d_model // LANES, PROLOGUE_ROWS, LANES), F32),
        ],
        compiler_params=pltpu.CompilerParams(
            dimension_semantics=("arbitrary", "arbitrary"),
            vmem_limit_bytes=VMEM_LIMIT_BYTES),
    )(x, gain, w_in, b_gate, ln_gain, ln_bias, rope_tab)


def _merge(old, new):
    acc0, den0, max0 = old
    acc1, den1, max1 = new
    m = jnp.maximum(max0, max1)
    w0 = jnp.exp2(max0 - m)
    w1 = jnp.exp2(max1 - m)
    return acc0 * w0 + acc1 * w1, den0 * w0 + den1 * w1, m


def _attn_group(q_ref, k_ref, v_ref, y_ref, acc_slab, den_slab, max_slab, tmp_ref,
                bias_ref, s_scr, p_scr, d, seq):
    st = DEINT_STRIDE
    sub_len = seq // d
    n_blk = sub_len // Q_BLOCK
    win = min(2 * Q_BLOCK, sub_len)
    ones = jnp.ones((win, HEAD_DIM), BF16)
    slabs = (acc_slab, den_slab, max_slab)
    bps = BLOCKS_PER_STEP

    def body(step, carry):
        blocks = []
        for e in range(bps):
            f = step * bps + e
            slot = f // n_blk
            q_off = (f % n_blk) * Q_BLOCK
            k_off = jnp.clip(q_off - RADIUS, 0, sub_len - win)
            blocks.append(dict(
                slot=slot, q_off=q_off, band=(q_off - k_off) // RADIUS,
                q0=pl.multiple_of(slot * sub_len + q_off, Q_BLOCK),
                k0=pl.multiple_of(slot * sub_len + k_off, RADIUS)))
        units = [(e, h) for e in range(bps) for h in range(HEADS_PER_GROUP)]

        for u, (e, h) in enumerate(units):
            blk = blocks[e]
            hs = slice(h * HEAD_DIM, (h + 1) * HEAD_DIM)
            s_scr[u, :, :win] = lax.dot_general(
                q_ref[pl.ds(blk["q0"], Q_BLOCK), hs], k_ref[pl.ds(blk["k0"], win), hs],
                (((1,), (1,)), ((), ())), preferred_element_type=F32)

        row_max = []
        for u, (e, h) in enumerate(units):
            s = s_scr[u, :, :win] + bias_ref[blocks[e]["band"], :, :win]
            m = jnp.max(s, axis=-1, keepdims=True)
            p_scr[u, :, :win] = jnp.exp2(s - m).astype(BF16)
            row_max.append(m)

        for u, (e, h) in enumerate(units):
            blk = blocks[e]
            hs = slice(h * HEAD_DIM, (h + 1) * HEAD_DIM)
            v_ext = jnp.concatenate([v_ref[pl.ds(blk["k0"], win), hs], ones], axis=1)
            oe = jnp.dot(p_scr[u, :, :win], v_ext, preferred_element_type=F32)
            new = (oe[:, :HEAD_DIM], oe[:, HEAD_DIM:],
                   jnp.broadcast_to(row_max[u], (Q_BLOCK, HEAD_DIM)))
            if d == st:
                rows = pl.ds(blk["q0"], Q_BLOCK)
                for slab, val in zip(slabs, new):
                    slab[h, rows, :] = val
            elif d == st * st:
                r1, r2 = blk["slot"] // st, blk["slot"] % st
                rows = pl.ds(r1 * (seq // st) + blk["q_off"] * st + r2, Q_BLOCK, stride=st)
                merged = _merge([slab[h, rows, :] for slab in slabs], new)
                for slab, val in zip(slabs, merged):
                    slab[h, rows, :] = val
            else:
                cnt = Q_BLOCK // st
                old = []
                for k, slab in enumerate(slabs):
                    t = len(slabs) * u + k
                    for r in range(st):
                        src = pl.multiple_of(r * (seq // st) + blk["q0"] // st, cnt)
                        tmp_ref[t, pl.ds(r, cnt, stride=st), :] = slab[h, pl.ds(src, cnt), :]
                    old.append(tmp_ref[t])
                acc, den, _ = _merge(old, new)
                y_ref[pl.ds(blk["q0"], Q_BLOCK), hs] = (acc / den).astype(BF16)
        return carry

    lax.fori_loop(0, d * n_blk // bps, body, 0)


def _attn_kernel(q_ref, k_ref, v_ref, y_ref, acc_slab, den_slab, max_slab, tmp_ref,
                 bias_ref, s_scr, p_scr, *, seq):
    g = pl.program_id(1)

    @pl.when((pl.program_id(0) == 0) & (g == 0))
    def _band_bias():
        row = lax.broadcasted_iota(jnp.int32, (Q_BLOCK, 2 * Q_BLOCK), 0)
        col = lax.broadcasted_iota(jnp.int32, (Q_BLOCK, 2 * Q_BLOCK), 1)
        for band in range(N_BANDS):
            valid = jnp.abs(row - col + band * RADIUS) <= RADIUS
            bias_ref[band] = jnp.where(valid, 0.0, MASK_VALUE).astype(F32)

    for step, gi in enumerate(GROUP_ORDER):
        @pl.when(g == step)
        def _group(d=ATTN_PATTERNS[gi][1]):
            _attn_group(q_ref, k_ref, v_ref, y_ref, acc_slab, den_slab, max_slab, tmp_ref,
                        bias_ref, s_scr, p_scr, d, seq)


def _attn_call(proj):
    _, bsz, seq, _ = proj.shape
    n_slabs = HEADS_PER_GROUP
    n_units = BLOCKS_PER_STEP * HEADS_PER_GROUP
    assert all(GROUP_ORDER[s] == (s + GROUP_ORDER[0]) % N_GROUPS for s in range(N_GROUPS))

    def spec(piece):
        return pl.BlockSpec(
            (None, None, seq, PIECE),
            lambda b, g: (TILE_QKV + (g + GROUP_ORDER[0]) % N_GROUPS, b, 0, piece))

    return pl.pallas_call(
        functools.partial(_attn_kernel, seq=seq),
        name="attn",
        grid=(bsz, N_GROUPS),
        in_specs=[spec(0), spec(1), spec(2)],
        out_specs=pl.BlockSpec((None, seq, GROUP_WIDTH), lambda b, g: (b, 0, 0)),
        out_shape=jax.ShapeDtypeStruct((bsz, seq, GROUP_WIDTH), BF16),
        scratch_shapes=[
            pltpu.VMEM((n_slabs, seq, LANES), F32),
            pltpu.VMEM((n_slabs, seq, LANES), F32),
            pltpu.VMEM((n_slabs, seq, LANES), F32),
            pltpu.VMEM((3 * n_units, Q_BLOCK, LANES), F32),
            pltpu.VMEM((N_BANDS, Q_BLOCK, 2 * Q_BLOCK), F32),
            pltpu.VMEM((n_units, Q_BLOCK, 2 * Q_BLOCK), F32),
            pltpu.VMEM((n_units, Q_BLOCK, 2 * Q_BLOCK), BF16),
        ],
        compiler_params=pltpu.CompilerParams(
            dimension_semantics=("arbitrary", "arbitrary"),
            vmem_limit_bytes=VMEM_LIMIT_BYTES),
    )(proj, proj, proj)


def _mix_kernel(zu_ref, zv_ref, ga_ref, gb_ref, yb_ref, x_ref,
                ws_ref, bsp_ref, wa_ref, wb_ref, wo_ref, gain_ref, o_ref, ya_ref):
    group_dim = GMLP_WIDTH // GMLP_GROUPS
    for c in range(0, MIX_ROWS // CHUNK, 2):
        rows0 = slice(c * CHUNK, (c + 1) * CHUNK)
        rows1 = slice((c + 1) * CHUNK, (c + 2) * CHUNK)
        sv = []
        for g in range(GMLP_GROUPS):
            cols = slice(g * group_dim, (g + 1) * group_dim)
            pair = jnp.concatenate([zv_ref[rows0, cols], zv_ref[rows1, cols]], axis=1)
            sv.append(jnp.dot(ws_ref[g], pair, preferred_element_type=F32))
        for k, rows in enumerate((rows0, rows1)):
            sv_k = jnp.concatenate(
                [s[:, k * group_dim:(k + 1) * group_dim] for s in sv], axis=1)
            ya_ref[rows, :] = (zu_ref[rows, :].astype(F32) * (sv_k + bsp_ref[...])).astype(BF16)
    a = jnp.dot(ya_ref[...], wa_ref[...], preferred_element_type=F32)
    b = jnp.dot(yb_ref[...], wb_ref[...], preferred_element_type=F32)
    merged = ga_ref[...].astype(F32) * a + gb_ref[...].astype(F32) * b
    mix = jnp.dot(merged.astype(BF16), wo_ref[...], preferred_element_type=F32)
    ms = jnp.mean(mix * mix, axis=-1, keepdims=True)
    o_ref[...] = x_ref[...] + mix * lax.rsqrt(ms + NORM_EPS) * gain_ref[...]


def _mix_call(proj, y_b, x, w_spatial, b_sp, w_a, w_b, w_out, gain):
    bsz, seq, d_model = x.shape

    def piece(tile, first_piece, n_pieces):
        width = n_pieces * PIECE
        return pl.BlockSpec((None, None, MIX_ROWS, width),
                            lambda b, i: (tile, b, i, first_piece // n_pieces))

    def whole(a):
        return pl.BlockSpec(a.shape, lambda b, i: (0,) * a.ndim)

    return pl.pallas_call(
        _mix_kernel,
        name="mix",
        grid=(bsz, seq // MIX_ROWS),
        in_specs=[
            piece(TILE_A, 2, 1), piece(TILE_B, 2, 1), piece(TILE_A, 0, 2), piece(TILE_B, 0, 2),
            pl.BlockSpec((None, MIX_ROWS, GROUP_WIDTH), lambda b, i: (b, i, 0)),
            pl.BlockSpec((None, MIX_ROWS, d_model), lambda b, i: (b, i, 0)),
            whole(w_spatial), whole(b_sp), whole(w_a), whole(w_b), whole(w_out), whole(gain),
        ],
        out_specs=pl.BlockSpec((None, MIX_ROWS, d_model), lambda b, i: (b, i, 0)),
        out_shape=jax.ShapeDtypeStruct((bsz, seq, d_model), F32),
        scratch_shapes=[pltpu.VMEM((MIX_ROWS, GMLP_WIDTH), BF16)],
        compiler_params=pltpu.CompilerParams(
            dimension_semantics=("parallel", "parallel"),
            vmem_limit_bytes=VMEM_LIMIT_BYTES),
    )(proj, proj, proj, proj, y_b, x, w_spatial, b_sp, w_a, w_b, w_out, gain)


def _mlp_kernel(h_ref, gpre_ref, wu_ref, wd_ref, gpost_ref, o_ref, n_ref, acc_ref, *, d_ff):
    h = h_ref[...]
    ms = jnp.mean(h * h, axis=-1, keepdims=True)
    n_ref[...] = (h * lax.rsqrt(ms + NORM_EPS) * gpre_ref[...]).astype(BF16)
    for c in range(d_ff // MLP_FF_CHUNK):
        cols = slice(c * MLP_FF_CHUNK, (c + 1) * MLP_FF_CHUNK)
        up = jnp.dot(n_ref[...], wu_ref[:, cols], preferred_element_type=F32)
        hid = jnp.square(jnp.maximum(up, 0.0)).astype(BF16)
        part = jnp.dot(hid, wd_ref[cols, :], preferred_element_type=F32)
        if c == 0:
            acc_ref[...] = part
        else:
            acc_ref[...] += part
    out = acc_ref[...]
    ms2 = jnp.mean(out * out, axis=-1, keepdims=True)
    o_ref[...] = h_ref[...] + out * lax.rsqrt(ms2 + NORM_EPS) * gpost_ref[...]


def _mlp_call(h, gain_pre, w_up, w_down, gain_post):
    bsz, seq, d_model = h.shape
    d_ff = w_up.shape[1]
    rows = bsz * seq
    h2 = h.reshape(rows, d_model)

    def whole(a):
        return pl.BlockSpec(a.shape, lambda i: (0,) * a.ndim, pipeline_mode=pl.Buffered(1))

    out = pl.pallas_call(
        functools.partial(_mlp_kernel, d_ff=d_ff),
        name="mlp",
        grid=(rows // MLP_ROWS,),
        in_specs=[
            pl.BlockSpec((MLP_ROWS, d_model), lambda i: (i, 0)),
            whole(gain_pre), whole(w_up), whole(w_down), whole(gain_post),
        ],
        out_specs=pl.BlockSpec((MLP_ROWS, d_model), lambda i: (i, 0)),
        out_shape=jax.ShapeDtypeStruct((rows, d_model), F32),
        scratch_shapes=[
            pltpu.VMEM((MLP_ROWS, d_model), BF16),
            pltpu.VMEM((MLP_ROWS, d_model), F32),
        ],
        compiler_params=pltpu.CompilerParams(
            dimension_semantics=("parallel",),
            vmem_limit_bytes=VMEM_LIMIT_BYTES),
    )(h2, gain_pre, w_up, w_down, gain_post)
    return out.reshape(bsz, seq, d_model)


def _rope_lane_order():
    half = ROPE_DIM // 2
    fill = HEAD_DIM // 2 - half
    return (list(range(half)) + list(range(ROPE_DIM, ROPE_DIM + fill))
            + list(range(half, ROPE_DIM)) + list(range(ROPE_DIM + fill, HEAD_DIM)))


def _source_piece(i):
    n_attn = 3 * N_GROUPS
    first_gate = 2 + n_attn
    in_qkv = i >= TILE_QKV * PIECES_PER_TILE
    a, c = i // PIECES_PER_TILE, i % PIECES_PER_TILE
    head_tiles = jnp.where(c < 2, first_gate + 2 * a + c, a)
    g, section = a - TILE_QKV, c
    return jnp.where(in_qkv, 2 + N_GROUPS * section + g, head_tiles)


def _wprep_kernel(w_ref, sel_ref, o_ref):
    i = pl.program_id(0)
    is_qk = (i >= TILE_QKV * PIECES_PER_TILE) & (i % PIECES_PER_TILE < 2)

    @pl.when(is_qk)
    def _reorder():
        for h in range(HEADS_PER_GROUP):
            hs = slice(h * HEAD_DIM, (h + 1) * HEAD_DIM)
            o_ref[:, hs] = jnp.dot(w_ref[:, hs].astype(BF16), sel_ref[...],
                                   preferred_element_type=F32).astype(BF16)

    @pl.when(jnp.logical_not(is_qk))
    def _copy():
        o_ref[...] = w_ref[...].astype(BF16)


def _regroup_w_in(w_in):
    d_model, d_in = w_in.shape
    select = np.zeros((HEAD_DIM, HEAD_DIM), np.float32)
    select[_rope_lane_order(), np.arange(HEAD_DIM)] = 1.0
    return pl.pallas_call(
        _wprep_kernel,
        name="wprep",
        grid=(d_in // PIECE,),
        in_specs=[
            pl.BlockSpec((d_model, PIECE), lambda i: (0, _source_piece(i))),
            pl.BlockSpec((HEAD_DIM, HEAD_DIM), lambda i: (0, 0)),
        ],
        out_specs=pl.BlockSpec((d_model, PIECE), lambda i: (0, i)),
        out_shape=jax.ShapeDtypeStruct((d_model, d_in), BF16),
        compiler_params=pltpu.CompilerParams(
            dimension_semantics=("parallel",),
            vmem_limit_bytes=VMEM_LIMIT_BYTES),
    )(w_in, jnp.asarray(select, BF16))


def _rope_tables(seq):
    half = ROPE_DIM // 2
    fill = HEAD_DIM // 2 - half
    inv_freq = np.float32(ROPE_THETA) ** (-np.arange(0, ROPE_DIM, 2, dtype=np.float32) / ROPE_DIM)
    ang = np.arange(seq, dtype=np.float32)[:, None] * inv_freq[None, :].astype(np.float32)
    cos, sin = np.cos(ang), np.sin(ang)
    ones = np.ones((seq, fill), np.float32)
    cos_l = np.concatenate([cos, ones, cos, ones], axis=1)
    sin_l = np.concatenate([-sin, 0.0 * ones, sin, 0.0 * ones], axis=1)
    nat = np.stack([cos_l, sin_l])
    per_group = []
    for _, d in ATTN_PATTERNS:
        by_residue = nat.reshape(2, seq // d, d, HEAD_DIM).transpose(0, 2, 1, 3)
        residues = [_slot_residue(slot, d) for slot in range(d)]
        per_group.append(by_residue[:, residues].reshape(2, seq, HEAD_DIM))
    return jnp.asarray(np.stack(per_group), F32)


def _layer(h, norm_mix_pre, w_in, b_gate, ln_v_gain, ln_v_bias, w_spatial, b_spatial,
           w_branch_a, w_branch_b, w_out, norm_mix_post, norm_mlp_pre, w_up, w_down,
           norm_mlp_post):
    seq = h.shape[1]
    row = lambda v: v.reshape(1, -1).astype(F32)
    proj = _proj_call(h, row(norm_mix_pre), _regroup_w_in(w_in), row(b_gate), row(ln_v_gain),
                      row(ln_v_bias), _rope_tables(seq))
    y_b = _attn_call(proj)
    b_sp = jnp.repeat(b_spatial.T.astype(F32), GMLP_WIDTH // GMLP_GROUPS, axis=1)
    h = _mix_call(proj, y_b, h, w_spatial.astype(BF16), b_sp, w_branch_a.astype(BF16),
                  w_branch_b.astype(BF16), w_out.astype(BF16), row(norm_mix_post))
    return _mlp_call(h, row(norm_mlp_pre), w_up.astype(BF16), w_down.astype(BF16),
                     row(norm_mlp_post))


def kernel(x, norm_mix_pre, w_in, b_gate, ln_v_gain, ln_v_bias, w_spatial, b_spatial,
           w_branch_a, w_branch_b, w_out, norm_mix_post, norm_mlp_pre, w_up, w_down,
           norm_mlp_post):
    h = x
    for l in range(w_in.shape[0]):
        h = _layer(h, norm_mix_pre[l], w_in[l], b_gate[l], ln_v_gain[l], ln_v_bias[l],
                   w_spatial[l], b_spatial[l], w_branch_a[l], w_branch_b[l], w_out[l],
                   norm_mix_post[l], norm_mlp_pre[l], w_up[l], w_down[l], norm_mlp_post[l])
    return h
```

```python
import functools
import math

import jax
import jax.numpy as jnp
import numpy as np
from jax import lax
from jax.experimental import pallas as pl
from jax.experimental.pallas import tpu as pltpu

GMLP_WIDTH = 512
GMLP_GROUPS = 4
CHUNK = 128
ATTN_PATTERNS = ((128, 1), (512, 4), (2048, 16))
N_GROUPS = len(ATTN_PATTERNS)
HEADS_PER_GROUP = 4
HEAD_DIM = 128
GROUP_WIDTH = HEADS_PER_GROUP * HEAD_DIM
ROPE_DIM = HEAD_DIM // 4
ROPE_THETA = 500000.0
NORM_EPS = 1e-6
MASK_VALUE = -1e30
LOG2E = math.log2(math.e)
QK_SCALE = LOG2E / math.sqrt(HEAD_DIM)
RADIUS = 64
assert all(w // (2 * d) == RADIUS for w, d in ATTN_PATTERNS)
DEINT_STRIDE = 4
assert tuple(d for _, d in ATTN_PATTERNS) == (1, DEINT_STRIDE, DEINT_STRIDE ** 2)


def _slot_residue(slot, d):
    if d == DEINT_STRIDE ** 2:
        return DEINT_STRIDE * (slot % DEINT_STRIDE) + slot // DEINT_STRIDE
    return slot


LANES = 128
VMEM_LIMIT_BYTES = 56 * 1024 * 1024

PIECE = 512
PIECES_PER_TILE = 3
TN = PIECES_PER_TILE * PIECE
PROLOGUE_ROWS = 256
PROJ_ROWS = 256
Q_BLOCK = 128
BLOCKS_PER_STEP = 16
TMP_UNITS = 16
N_BANDS = 3
GROUP_ORDER = (1, 2, 0)
MIX_ROWS = 1024
MLP_ROWS = 1024
MLP_FF_CHUNK = 1024

BF16 = jnp.bfloat16
F32 = jnp.float32

TILE_A, TILE_B, TILE_QKV = 0, 1, 2
N_TILES = TILE_QKV + N_GROUPS


def _gelu_tanh(x):
    c = math.sqrt(2.0 / math.pi)
    return x * (0.5 * (1.0 + jnp.tanh(c * (x + 0.044715 * (x * x * x)))))


def _sigmoid(x):
    return 1.0 / (1.0 + jnp.exp(-x))


def _proj_kernel(x_ref, gain_ref, w_ref, bg_ref, lng_ref, lnb_ref, rope_ref, o_ref,
                 lhs_ref, slab_ref, *, seq, d_model):
    j = pl.program_id(1)
    n_slabs = d_model // LANES

    @pl.when(j == 0)
    def _prologue():
        st = DEINT_STRIDE
        cnt1 = PROLOGUE_ROWS // st
        cnt2 = cnt1 // st

        def body(c, carry):
            r0 = pl.multiple_of(c * PROLOGUE_ROWS, PROLOGUE_ROWS)
            xc = x_ref[pl.ds(r0, PROLOGUE_ROWS), :]
            ms = jnp.mean(xc * xc, axis=-1, keepdims=True)
            n = xc * lax.rsqrt(ms + NORM_EPS) * gain_ref[...]
            lhs_ref[0, pl.ds(r0, PROLOGUE_ROWS), :] = n.astype(BF16)
            for s in range(n_slabs):
                lanes = slice(s * LANES, (s + 1) * LANES)
                slab_ref[0, s] = n[:, lanes]
                for r1 in range(st):
                    piece = slab_ref[0, s, pl.ds(r1, cnt1, stride=st), :]
                    dst1 = pl.multiple_of(r1 * (seq // st) + c * cnt1, cnt1)
                    lhs_ref[1, pl.ds(dst1, cnt1), lanes] = piece.astype(BF16)
                    slab_ref[1, s, r1 * cnt1:(r1 + 1) * cnt1, :] = piece
                for r1 in range(st):
                    for r2 in range(st):
                        sub = slab_ref[1, s, pl.ds(r1 * cnt1 + r2, cnt2, stride=st), :]
                        slot = r1 * st + r2
                        dst2 = pl.multiple_of(slot * (seq // (st * st)) + c * cnt2, cnt2)
                        lhs_ref[2, pl.ds(dst2, cnt2), lanes] = sub.astype(BF16)
            return carry

        lax.fori_loop(0, seq // PROLOGUE_ROWS, body, 0)

    def matmul_rows(lhs_idx, epilogues):
        for c in range(seq // PROJ_ROWS):
            r0 = c * PROJ_ROWS
            acc = jnp.dot(lhs_ref[lhs_idx, pl.ds(r0, PROJ_ROWS), :], w_ref[...],
                          preferred_element_type=F32)
            for c0, width, fn in epilogues:
                o_ref[pl.ds(r0, PROJ_ROWS), c0:c0 + width] = (
                    fn(acc[:, c0:c0 + width], r0).astype(BF16))

    def gates(acc, r0):
        return _sigmoid(acc + bg_ref[...])

    def layer_norm_gelu(acc, r0):
        z = _gelu_tanh(acc)
        mu = jnp.mean(z, axis=-1, keepdims=True)
        zc = z - mu
        var = jnp.mean(zc * zc, axis=-1, keepdims=True)
        return zc * lax.rsqrt(var + NORM_EPS) * lng_ref[...] + lnb_ref[...]

    def rotary(scale):
        def fn(acc, r0):
            cos = rope_ref[0, pl.ds(r0, PROJ_ROWS), :]
            sin = rope_ref[1, pl.ds(r0, PROJ_ROWS), :]
            heads = []
            for h in range(HEADS_PER_GROUP):
                t = acc[:, h * HEAD_DIM:(h + 1) * HEAD_DIM]
                out = t * cos + pltpu.roll(t, HEAD_DIM // 2, 1) * sin
                heads.append(out if scale is None else out * scale)
            return jnp.concatenate(heads, axis=1)

        return fn

    @pl.when(j == TILE_A)
    def _tile_a():
        matmul_rows(0, [(0, 2 * PIECE, gates),
                        (2 * PIECE, PIECE, lambda acc, r0: _gelu_tanh(acc))])

    @pl.when(j == TILE_B)
    def _tile_b():
        matmul_rows(0, [(0, 2 * PIECE, gates), (2 * PIECE, PIECE, layer_norm_gelu)])

    @pl.when(j >= TILE_QKV)
    def _tile_qkv():
        matmul_rows(j - TILE_QKV, [(0, PIECE, rotary(QK_SCALE)),
                                   (PIECE, PIECE, rotary(None)),
                                   (2 * PIECE, PIECE, lambda acc, r0: acc)])


def _proj_call(x, gain, w_in, b_gate, ln_gain, ln_bias, rope_tab):
    bsz, seq, d_model = x.shape
    assert w_in.shape[1] == N_TILES * TN
    kern = functools.partial(_proj_kernel, seq=seq, d_model=d_model)
    return pl.pallas_call(
        kern,
        name="proj",
        grid=(bsz, N_TILES),
        in_specs=[
            pl.BlockSpec((None, seq, d_model), lambda b, j: (b, 0, 0)),
            pl.BlockSpec((1, d_model), lambda b, j: (0, 0)),
            pl.BlockSpec((d_model, TN), lambda b, j: (0, j)),
            pl.BlockSpec((1, 2 * PIECE), lambda b, j: (0, jnp.minimum(j, TILE_B))),
            pl.BlockSpec((1, PIECE), lambda b, j: (0, 0)),
            pl.BlockSpec((1, PIECE), lambda b, j: (0, 0)),
            pl.BlockSpec((None, 2, seq, HEAD_DIM),
                         lambda b, j: (jnp.maximum(j - TILE_QKV, 0), 0, 0, 0)),
        ],
        out_specs=pl.BlockSpec((None, None, seq, TN), lambda b, j: (j, b, 0, 0)),
        out_shape=jax.ShapeDtypeStruct((N_TILES, bsz, seq, TN), BF16),
        scratch_shapes=[
            pltpu.VMEM((N_GROUPS, seq, d_model), BF16),
            pltpu.VMEM((2, d_model // LANES, PROLOGUE_ROWS, LANES), F32),
        ],
        compiler_params=pltpu.CompilerParams(
            dimension_semantics=("arbitrary", "arbitrary"),
            vmem_limit_bytes=VMEM_LIMIT_BYTES),
    )(x, gain, w_in, b_gate, ln_gain, ln_bias, rope_tab)


def _merge(old, new):
    acc0, den0, max0 = old
    acc1, den1, max1 = new
    m = jnp.maximum(max0, max1)
    w0 = jnp.exp2(max0 - m)
    w1 = jnp.exp2(max1 - m)
    return acc0 * w0 + acc1 * w1, den0 * w0 + den1 * w1, m


def _attn_group(q_ref, k_ref, v_ref, y_ref, acc_slab, den_slab, max_slab, tmp_ref,
                bias_ref, s_scr, p_scr, d, seq):
    st = DEINT_STRIDE
    sub_len = seq // d
    n_blk = sub_len // Q_BLOCK
    win = min(2 * Q_BLOCK, sub_len)
    ones = jnp.ones((win, HEAD_DIM), BF16)
    slabs = (acc_slab, den_slab, max_slab)
    bps = BLOCKS_PER_STEP

    def body(step, carry):
        blocks = []
        for e in range(bps):
            f = step * bps + e
            slot = f // n_blk
            q_off = (f % n_blk) * Q_BLOCK
            k_off = jnp.clip(q_off - RADIUS, 0, sub_len - win)
            blocks.append(dict(
                slot=slot, q_off=q_off, band=(q_off - k_off) // RADIUS,
                q0=pl.multiple_of(slot * sub_len + q_off, Q_BLOCK),
                k0=pl.multiple_of(slot * sub_len + k_off, RADIUS)))
        units = [(e, h) for e in range(bps) for h in range(HEADS_PER_GROUP)]

        for u, (e, h) in enumerate(units):
            blk = blocks[e]
            hs = slice(h * HEAD_DIM, (h + 1) * HEAD_DIM)
            s_scr[u, :, :win] = lax.dot_general(
                q_ref[pl.ds(blk["q0"], Q_BLOCK), hs], k_ref[pl.ds(blk["k0"], win), hs],
                (((1,), (1,)), ((), ())), preferred_element_type=F32)

        row_max = []
        for u, (e, h) in enumerate(units):
            s = s_scr[u, :, :win] + bias_ref[blocks[e]["band"], :, :win]
            m = jnp.max(s, axis=-1, keepdims=True)
            p_scr[u, :, :win] = jnp.exp2(s - m).astype(BF16)
            row_max.append(m)

        for u, (e, h) in enumerate(units):
            blk = blocks[e]
            hs = slice(h * HEAD_DIM, (h + 1) * HEAD_DIM)
            v_ext = jnp.concatenate([v_ref[pl.ds(blk["k0"], win), hs], ones], axis=1)
            oe = jnp.dot(p_scr[u, :, :win], v_ext, preferred_element_type=F32)
            new = (oe[:, :HEAD_DIM], oe[:, HEAD_DIM:],
                   jnp.broadcast_to(row_max[u], (Q_BLOCK, HEAD_DIM)))
            if d == st:
                rows = pl.ds(blk["q0"], Q_BLOCK)
                for slab, val in zip(slabs, new):
                    slab[h, rows, :] = val
            elif d == st * st:
                r1, r2 = blk["slot"] // st, blk["slot"] % st
                rows = pl.ds(r1 * (seq // st) + blk["q_off"] * st + r2, Q_BLOCK, stride=st)
                merged = _merge([slab[h, rows, :] for slab in slabs], new)
                for slab, val in zip(slabs, merged):
                    slab[h, rows, :] = val
            else:
                cnt = Q_BLOCK // st
                old = []
                for k, slab in enumerate(slabs):
                    t = len(slabs) * (u % TMP_UNITS) + k
                    for r in range(st):
                        src = pl.multiple_of(r * (seq // st) + blk["q0"] // st, cnt)
                        tmp_ref[t, pl.ds(r, cnt, stride=st), :] = slab[h, pl.ds(src, cnt), :]
                    old.append(tmp_ref[t])
                acc, den, _ = _merge(old, new)
                y_ref[pl.ds(blk["q0"], Q_BLOCK), hs] = (acc / den).astype(BF16)
        return carry

    lax.fori_loop(0, d * n_blk // bps, body, 0)


def _attn_kernel(q_ref, k_ref, v_ref, y_ref, acc_slab, den_slab, max_slab, tmp_ref,
                 bias_ref, s_scr, p_scr, *, seq):
    g = pl.program_id(1)

    @pl.when((pl.program_id(0) == 0) & (g == 0))
    def _band_bias():
        row = lax.broadcasted_iota(jnp.int32, (Q_BLOCK, 2 * Q_BLOCK), 0)
        col = lax.broadcasted_iota(jnp.int32, (Q_BLOCK, 2 * Q_BLOCK), 1)
        for band in range(N_BANDS):
            valid = jnp.abs(row - col + band * RADIUS) <= RADIUS
            bias_ref[band] = jnp.where(valid, 0.0, MASK_VALUE).astype(F32)

    for step, gi in enumerate(GROUP_ORDER):
        @pl.when(g == step)
        def _group(d=ATTN_PATTERNS[gi][1]):
            _attn_group(q_ref, k_ref, v_ref, y_ref, acc_slab, den_slab, max_slab, tmp_ref,
                        bias_ref, s_scr, p_scr, d, seq)


def _attn_call(proj):
    _, bsz, seq, _ = proj.shape
    n_slabs = HEADS_PER_GROUP
    n_units = BLOCKS_PER_STEP * HEADS_PER_GROUP
    assert all(GROUP_ORDER[s] == (s + GROUP_ORDER[0]) % N_GROUPS for s in range(N_GROUPS))

    def spec(piece):
        return pl.BlockSpec(
            (None, None, seq, PIECE),
            lambda b, g: (TILE_QKV + (g + GROUP_ORDER[0]) % N_GROUPS, b, 0, piece))

    return pl.pallas_call(
        functools.partial(_attn_kernel, seq=seq),
        name="attn",
        grid=(bsz, N_GROUPS),
        in_specs=[spec(0), spec(1), spec(2)],
        out_specs=pl.BlockSpec((None, seq, GROUP_WIDTH), lambda b, g: (b, 0, 0)),
        out_shape=jax.ShapeDtypeStruct((bsz, seq, GROUP_WIDTH), BF16),
        scratch_shapes=[
            pltpu.VMEM((n_slabs, seq, LANES), F32),
            pltpu.VMEM((n_slabs, seq, LANES), F32),
            pltpu.VMEM((n_slabs, seq, LANES), F32),
            pltpu.VMEM((3 * TMP_UNITS, Q_BLOCK, LANES), F32),
            pltpu.VMEM((N_BANDS, Q_BLOCK, 2 * Q_BLOCK), F32),
            pltpu.VMEM((n_units, Q_BLOCK, 2 * Q_BLOCK), F32),
            pltpu.VMEM((n_units, Q_BLOCK, 2 * Q_BLOCK), BF16),
        ],
        compiler_params=pltpu.CompilerParams(
            dimension_semantics=("arbitrary", "arbitrary"),
            vmem_limit_bytes=VMEM_LIMIT_BYTES),
    )(proj, proj, proj)


def _mix_kernel(zu_ref, zv_ref, ga_ref, gb_ref, yb_ref, x_ref,
                ws_ref, bsp_ref, wa_ref, wb_ref, wo_ref, gain_ref, o_ref, ya_ref):
    group_dim = GMLP_WIDTH // GMLP_GROUPS
    for c in range(0, MIX_ROWS // CHUNK, 2):
        rows0 = slice(c * CHUNK, (c + 1) * CHUNK)
        rows1 = slice((c + 1) * CHUNK, (c + 2) * CHUNK)
        sv = []
        for g in range(GMLP_GROUPS):
            cols = slice(g * group_dim, (g + 1) * group_dim)
            pair = jnp.concatenate([zv_ref[rows0, cols], zv_ref[rows1, cols]], axis=1)
            sv.append(jnp.dot(ws_ref[g], pair, preferred_element_type=F32))
        for k, rows in enumerate((rows0, rows1)):
            sv_k = jnp.concatenate(
                [s[:, k * group_dim:(k + 1) * group_dim] for s in sv], axis=1)
            ya_ref[rows, :] = (zu_ref[rows, :].astype(F32) * (sv_k + bsp_ref[...])).astype(BF16)
    a = jnp.dot(ya_ref[...], wa_ref[...], preferred_element_type=F32)
    b = jnp.dot(yb_ref[...], wb_ref[...], preferred_element_type=F32)
    merged = ga_ref[...].astype(F32) * a + gb_ref[...].astype(F32) * b
    mix = jnp.dot(merged.astype(BF16), wo_ref[...], preferred_element_type=F32)
    ms = jnp.mean(mix * mix, axis=-1, keepdims=True)
    o_ref[...] = x_ref[...] + mix * lax.rsqrt(ms + NORM_EPS) * gain_ref[...]


def _mix_call(proj, y_b, x, w_spatial, b_sp, w_a, w_b, w_out, gain):
    bsz, seq, d_model = x.shape

    def piece(tile, first_piece, n_pieces):
        width = n_pieces * PIECE
        return pl.BlockSpec((None, None, MIX_ROWS, width),
                            lambda b, i: (tile, b, i, first_piece // n_pieces))

    def whole(a):
        return pl.BlockSpec(a.shape, lambda b, i: (0,) * a.ndim)

    return pl.pallas_call(
        _mix_kernel,
        name="mix",
        grid=(bsz, seq // MIX_ROWS),
        in_specs=[
            piece(TILE_A, 2, 1), piece(TILE_B, 2, 1), piece(TILE_A, 0, 2), piece(TILE_B, 0, 2),
            pl.BlockSpec((None, MIX_ROWS, GROUP_WIDTH), lambda b, i: (b, i, 0)),
            pl.BlockSpec((None, MIX_ROWS, d_model), lambda b, i: (b, i, 0)),
            whole(w_spatial), whole(b_sp), whole(w_a), whole(w_b), whole(w_out), whole(gain),
        ],
        out_specs=pl.BlockSpec((None, MIX_ROWS, d_model), lambda b, i: (b, i, 0)),
        out_shape=jax.ShapeDtypeStruct((bsz, seq, d_model), F32),
        scratch_shapes=[pltpu.VMEM((MIX_ROWS, GMLP_WIDTH), BF16)],
        compiler_params=pltpu.CompilerParams(
            dimension_semantics=("parallel", "parallel"),
            vmem_limit_bytes=VMEM_LIMIT_BYTES),
    )(proj, proj, proj, proj, y_b, x, w_spatial, b_sp, w_a, w_b, w_out, gain)


def _mlp_kernel(h_ref, gpre_ref, wu_ref, wd_ref, gpost_ref, o_ref, n_ref, acc_ref, *, d_ff):
    h = h_ref[...]
    ms = jnp.mean(h * h, axis=-1, keepdims=True)
    n_ref[...] = (h * lax.rsqrt(ms + NORM_EPS) * gpre_ref[...]).astype(BF16)
    for c in range(d_ff // MLP_FF_CHUNK):
        cols = slice(c * MLP_FF_CHUNK, (c + 1) * MLP_FF_CHUNK)
        up = jnp.dot(n_ref[...], wu_ref[:, cols], preferred_element_type=F32)
        hid = jnp.square(jnp.maximum(up, 0.0)).astype(BF16)
        part = jnp.dot(hid, wd_ref[cols, :], preferred_element_type=F32)
        if c == 0:
            acc_ref[...] = part
        else:
            acc_ref[...] += part
    out = acc_ref[...]
    ms2 = jnp.mean(out * out, axis=-1, keepdims=True)
    o_ref[...] = h_ref[...] + out * lax.rsqrt(ms2 + NORM_EPS) * gpost_ref[...]


def _mlp_call(h, gain_pre, w_up, w_down, gain_post):
    bsz, seq, d_model = h.shape
    d_ff = w_up.shape[1]
    rows = bsz * seq
    h2 = h.reshape(rows, d_model)

    def whole(a):
        return pl.BlockSpec(a.shape, lambda i: (0,) * a.ndim, pipeline_mode=pl.Buffered(1))

    out = pl.pallas_call(
        functools.partial(_mlp_kernel, d_ff=d_ff),
        name="mlp",
        grid=(rows // MLP_ROWS,),
        in_specs=[
            pl.BlockSpec((MLP_ROWS, d_model), lambda i: (i, 0)),
            whole(gain_pre), whole(w_up), whole(w_down), whole(gain_post),
        ],
        out_specs=pl.BlockSpec((MLP_ROWS, d_model), lambda i: (i, 0)),
        out_shape=jax.ShapeDtypeStruct((rows, d_model), F32),
        scratch_shapes=[
            pltpu.VMEM((MLP_ROWS, d_model), BF16),
            pltpu.VMEM((MLP_ROWS, d_model), F32),
        ],
        compiler_params=pltpu.CompilerParams(
            dimension_semantics=("parallel",),
            vmem_limit_bytes=VMEM_LIMIT_BYTES),
    )(h2, gain_pre, w_up, w_down, gain_post)
    return out.reshape(bsz, seq, d_model)


def _rope_lane_order():
    half = ROPE_DIM // 2
    fill = HEAD_DIM // 2 - half
    return (list(range(half)) + list(range(ROPE_DIM, ROPE_DIM + fill))
            + list(range(half, ROPE_DIM)) + list(range(ROPE_DIM + fill, HEAD_DIM)))


def _source_piece(i):
    n_attn = 3 * N_GROUPS
    first_gate = 2 + n_attn
    in_qkv = i >= TILE_QKV * PIECES_PER_TILE
    a, c = i // PIECES_PER_TILE, i % PIECES_PER_TILE
    head_tiles = jnp.where(c < 2, first_gate + 2 * a + c, a)
    g, section = a - TILE_QKV, c
    return jnp.where(in_qkv, 2 + N_GROUPS * section + g, head_tiles)


def _wprep_kernel(w_ref, sel_ref, o_ref):
    i = pl.program_id(0)
    is_qk = (i >= TILE_QKV * PIECES_PER_TILE) & (i % PIECES_PER_TILE < 2)

    @pl.when(is_qk)
    def _reorder():
        for h in range(HEADS_PER_GROUP):
            hs = slice(h * HEAD_DIM, (h + 1) * HEAD_DIM)
            o_ref[:, hs] = jnp.dot(w_ref[:, hs].astype(BF16), sel_ref[...],
                                   preferred_element_type=F32).astype(BF16)

    @pl.when(jnp.logical_not(is_qk))
    def _copy():
        o_ref[...] = w_ref[...].astype(BF16)


def _regroup_w_in(w_in):
    d_model, d_in = w_in.shape
    select = np.zeros((HEAD_DIM, HEAD_DIM), np.float32)
    select[_rope_lane_order(), np.arange(HEAD_DIM)] = 1.0
    return pl.pallas_call(
        _wprep_kernel,
        name="wprep",
        grid=(d_in // PIECE,),
        in_specs=[
            pl.BlockSpec((d_model, PIECE), lambda i: (0, _source_piece(i))),
            pl.BlockSpec((HEAD_DIM, HEAD_DIM), lambda i: (0, 0)),
        ],
        out_specs=pl.BlockSpec((d_model, PIECE), lambda i: (0, i)),
        out_shape=jax.ShapeDtypeStruct((d_model, d_in), BF16),
        compiler_params=pltpu.CompilerParams(
            dimension_semantics=("parallel",),
            vmem_limit_bytes=VMEM_LIMIT_BYTES),
    )(w_in, jnp.asarray(select, BF16))


def _rope_tables(seq):
    half = ROPE_DIM // 2
    fill = HEAD_DIM // 2 - half
    inv_freq = np.float32(ROPE_THETA) ** (-np.arange(0, ROPE_DIM, 2, dtype=np.float32) / ROPE_DIM)
    ang = np.arange(seq, dtype=np.float32)[:, None] * inv_freq[None, :].astype(np.float32)
    cos, sin = np.cos(ang), np.sin(ang)
    ones = np.ones((seq, fill), np.float32)
    cos_l = np.concatenate([cos, ones, cos, ones], axis=1)
    sin_l = np.concatenate([-sin, 0.0 * ones, sin, 0.0 * ones], axis=1)
    nat = np.stack([cos_l, sin_l])
    per_group = []
    for _, d in ATTN_PATTERNS:
        by_residue = nat.reshape(2, seq // d, d, HEAD_DIM).transpose(0, 2, 1, 3)
        residues = [_slot_residue(slot, d) for slot in range(d)]
        per_group.append(by_residue[:, residues].reshape(2, seq, HEAD_DIM))
    return jnp.asarray(np.stack(per_group), F32)


def _layer(h, norm_mix_pre, w_in, b_gate, ln_v_gain, ln_v_bias, w_spatial, b_spatial,
           w_branch_a, w_branch_b, w_out, norm_mix_post, norm_mlp_pre, w_up, w_down,
           norm_mlp_post):
    seq = h.shape[1]
    row = lambda v: v.reshape(1, -1).astype(F32)
    proj = _proj_call(h, row(norm_mix_pre), _regroup_w_in(w_in), row(b_gate), row(ln_v_gain),
                      row(ln_v_bias), _rope_tables(seq))
    y_b = _attn_call(proj)
    b_sp = jnp.repeat(b_spatial.T.astype(F32), GMLP_WIDTH // GMLP_GROUPS, axis=1)
    h = _mix_call(proj, y_b, h, w_spatial.astype(BF16), b_sp, w_branch_a.astype(BF16),
                  w_branch_b.astype(BF16), w_out.astype(BF16), row(norm_mix_post))
    return _mlp_call(h, row(norm_mlp_pre), w_up.astype(BF16), w_down.astype(BF16),
                     row(norm_mlp_post))


def kernel(x, norm_mix_pre, w_in, b_gate, ln_v_gain, ln_v_bias, w_spatial, b_spatial,
           w_branch_a, w_branch_b, w_out, norm_mix_post, norm_mlp_pre, w_up, w_down,
           norm_mlp_post):
    h = x
    for l in range(w_in.shape[0]):
        h = _layer(h, norm_mix_pre[l], w_in[l], b_gate[l], ln_v_gain[l], ln_v_bias[l],
                   w_spatial[l], b_spatial[l], w_branch_a[l], w_branch_b[l], w_out[l],
                   norm_mix_post[l], norm_mlp_pre[l], w_up[l], w_down[l], norm_mlp_post[l])
    return h
```

```python
import functools
import math

import jax
import jax.numpy as jnp
import numpy as np
from jax import lax
from jax.experimental import pallas as pl
from jax.experimental.pallas import tpu as pltpu

GMLP_WIDTH = 512
GMLP_GROUPS = 4
CHUNK = 128
ATTN_PATTERNS = ((128, 1), (512, 4), (2048, 16))
N_GROUPS = len(ATTN_PATTERNS)
HEADS_PER_GROUP = 4
HEAD_DIM = 128
GROUP_WIDTH = HEADS_PER_GROUP * HEAD_DIM
ROPE_DIM = HEAD_DIM // 4
ROPE_THETA = 500000.0
NORM_EPS = 1e-6
MASK_VALUE = -1e30
LOG2E = math.log2(math.e)
QK_SCALE = LOG2E / math.sqrt(HEAD_DIM)
RADIUS = 64
assert all(w // (2 * d) == RADIUS for w, d in ATTN_PATTERNS)
DEINT_STRIDE = 4
assert tuple(d for _, d in ATTN_PATTERNS) == (1, DEINT_STRIDE, DEINT_STRIDE ** 2)


def _slot_residue(slot, d):
    if d == DEINT_STRIDE ** 2:
        return DEINT_STRIDE * (slot % DEINT_STRIDE) + slot // DEINT_STRIDE
    return slot


LANES = 128
VMEM_LIMIT_BYTES = 56 * 1024 * 1024

PIECE = 512
PIECES_PER_TILE = 3
TN = PIECES_PER_TILE * PIECE
PROLOGUE_ROWS = 256
PROJ_ROWS = 128
Q_BLOCK = 128
BLOCKS_PER_STEP = 16
TMP_UNITS = 16
N_BANDS = 3
GROUP_ORDER = (1, 2, 0)
MIX_ROWS = 1024
MLP_ROWS = 1024
MLP_FF_CHUNK = 1024

BF16 = jnp.bfloat16
F32 = jnp.float32

TILE_A, TILE_B, TILE_QKV = 0, 1, 2
N_TILES = TILE_QKV + N_GROUPS


def _gelu_tanh(x):
    c = math.sqrt(2.0 / math.pi)
    return x * (0.5 * (1.0 + jnp.tanh(c * (x + 0.044715 * (x * x * x)))))


def _sigmoid(x):
    return 1.0 / (1.0 + jnp.exp(-x))


def _proj_kernel(x_ref, gain_ref, w_ref, bg_ref, lng_ref, lnb_ref, rope_ref, o_ref,
                 lhs_ref, slab_ref, *, seq, d_model):
    j = pl.program_id(1)
    n_slabs = d_model // LANES

    @pl.when(j == 0)
    def _prologue():
        st = DEINT_STRIDE
        cnt1 = PROLOGUE_ROWS // st
        cnt2 = cnt1 // st

        def body(c, carry):
            r0 = pl.multiple_of(c * PROLOGUE_ROWS, PROLOGUE_ROWS)
            xc = x_ref[pl.ds(r0, PROLOGUE_ROWS), :]
            ms = jnp.mean(xc * xc, axis=-1, keepdims=True)
            n = xc * lax.rsqrt(ms + NORM_EPS) * gain_ref[...]
            lhs_ref[0, pl.ds(r0, PROLOGUE_ROWS), :] = n.astype(BF16)
            for s in range(n_slabs):
                lanes = slice(s * LANES, (s + 1) * LANES)
                slab_ref[0, s] = n[:, lanes]
                for r1 in range(st):
                    piece = slab_ref[0, s, pl.ds(r1, cnt1, stride=st), :]
                    dst1 = pl.multiple_of(r1 * (seq // st) + c * cnt1, cnt1)
                    lhs_ref[1, pl.ds(dst1, cnt1), lanes] = piece.astype(BF16)
                    slab_ref[1, s, r1 * cnt1:(r1 + 1) * cnt1, :] = piece
                for r1 in range(st):
                    for r2 in range(st):
                        sub = slab_ref[1, s, pl.ds(r1 * cnt1 + r2, cnt2, stride=st), :]
                        slot = r1 * st + r2
                        dst2 = pl.multiple_of(slot * (seq // (st * st)) + c * cnt2, cnt2)
                        lhs_ref[2, pl.ds(dst2, cnt2), lanes] = sub.astype(BF16)
            return carry

        lax.fori_loop(0, seq // PROLOGUE_ROWS, body, 0)

    def matmul_rows(lhs_idx, epilogues):
        for c in range(seq // PROJ_ROWS):
            r0 = c * PROJ_ROWS
            acc = jnp.dot(lhs_ref[lhs_idx, pl.ds(r0, PROJ_ROWS), :], w_ref[...],
                          preferred_element_type=F32)
            for c0, width, fn in epilogues:
                o_ref[pl.ds(r0, PROJ_ROWS), c0:c0 + width] = (
                    fn(acc[:, c0:c0 + width], r0).astype(BF16))

    def gates(acc, r0):
        return _sigmoid(acc + bg_ref[...])

    def layer_norm_gelu(acc, r0):
        z = _gelu_tanh(acc)
        mu = jnp.mean(z, axis=-1, keepdims=True)
        zc = z - mu
        var = jnp.mean(zc * zc, axis=-1, keepdims=True)
        return zc * lax.rsqrt(var + NORM_EPS) * lng_ref[...] + lnb_ref[...]

    def rotary(scale):
        def fn(acc, r0):
            cos = rope_ref[0, pl.ds(r0, PROJ_ROWS), :]
            sin = rope_ref[1, pl.ds(r0, PROJ_ROWS), :]
            heads = []
            for h in range(HEADS_PER_GROUP):
                t = acc[:, h * HEAD_DIM:(h + 1) * HEAD_DIM]
                out = t * cos + pltpu.roll(t, HEAD_DIM // 2, 1) * sin
                heads.append(out if scale is None else out * scale)
            return jnp.concatenate(heads, axis=1)

        return fn

    @pl.when(j == TILE_A)
    def _tile_a():
        matmul_rows(0, [(0, 2 * PIECE, gates),
                        (2 * PIECE, PIECE, lambda acc, r0: _gelu_tanh(acc))])

    @pl.when(j == TILE_B)
    def _tile_b():
        matmul_rows(0, [(0, 2 * PIECE, gates), (2 * PIECE, PIECE, layer_norm_gelu)])

    @pl.when(j >= TILE_QKV)
    def _tile_qkv():
        matmul_rows(j - TILE_QKV, [(0, PIECE, rotary(QK_SCALE)),
                                   (PIECE, PIECE, rotary(None)),
                                   (2 * PIECE, PIECE, lambda acc, r0: acc)])


def _proj_call(x, gain, w_in, b_gate, ln_gain, ln_bias, rope_tab):
    bsz, seq, d_model = x.shape
    assert w_in.shape[1] == N_TILES * TN
    kern = functools.partial(_proj_kernel, seq=seq, d_model=d_model)
    return pl.pallas_call(
        kern,
        name="proj",
        grid=(bsz, N_TILES),
        in_specs=[
            pl.BlockSpec((None, seq, d_model), lambda b, j: (b, 0, 0)),
            pl.BlockSpec((1, d_model), lambda b, j: (0, 0)),
            pl.BlockSpec((d_model, TN), lambda b, j: (0, j)),
            pl.BlockSpec((1, 2 * PIECE), lambda b, j: (0, jnp.minimum(j, TILE_B))),
            pl.BlockSpec((1, PIECE), lambda b, j: (0, 0)),
            pl.BlockSpec((1, PIECE), lambda b, j: (0, 0)),
            pl.BlockSpec((None, 2, seq, HEAD_DIM),
                         lambda b, j: (jnp.maximum(j - TILE_QKV, 0), 0, 0, 0)),
        ],
        out_specs=pl.BlockSpec((None, None, seq, TN), lambda b, j: (j, b, 0, 0)),
        out_shape=jax.ShapeDtypeStruct((N_TILES, bsz, seq, TN), BF16),
        scratch_shapes=[
            pltpu.VMEM((N_GROUPS, seq, d_model), BF16),
            pltpu.VMEM((2, d_model // LANES, PROLOGUE_ROWS, LANES), F32),
        ],
        compiler_params=pltpu.CompilerParams(
            dimension_semantics=("arbitrary", "arbitrary"),
            vmem_limit_bytes=VMEM_LIMIT_BYTES),
    )(x, gain, w_in, b_gate, ln_gain, ln_bias, rope_tab)


def _merge(old, new):
    acc0, den0, max0 = old
    acc1, den1, max1 = new
    m = jnp.maximum(max0, max1)
    w0 = jnp.exp2(max0 - m)
    w1 = jnp.exp2(max1 - m)
    return acc0 * w0 + acc1 * w1, den0 * w0 + den1 * w1, m


def _attn_group(q_ref, k_ref, v_ref, y_ref, acc_slab, den_slab, max_slab, tmp_ref,
                bias_ref, s_scr, p_scr, d, seq):
    st = DEINT_STRIDE
    sub_len = seq // d
    n_blk = sub_len // Q_BLOCK
    win = min(2 * Q_BLOCK, sub_len)
    ones = jnp.ones((win, HEAD_DIM), BF16)
    slabs = (acc_slab, den_slab, max_slab)
    bps = BLOCKS_PER_STEP

    def body(step, carry):
        blocks = []
        for e in range(bps):
            f = step * bps + e
            slot = f // n_blk
            q_off = (f % n_blk) * Q_BLOCK
            k_off = jnp.clip(q_off - RADIUS, 0, sub_len - win)
            blocks.append(dict(
                slot=slot, q_off=q_off, band=(q_off - k_off) // RADIUS,
                q0=pl.multiple_of(slot * sub_len + q_off, Q_BLOCK),
                k0=pl.multiple_of(slot * sub_len + k_off, RADIUS)))
        units = [(e, h) for e in range(bps) for h in range(HEADS_PER_GROUP)]

        for u, (e, h) in enumerate(units):
            blk = blocks[e]
            hs = slice(h * HEAD_DIM, (h + 1) * HEAD_DIM)
            s_scr[u, :, :win] = lax.dot_general(
                q_ref[pl.ds(blk["q0"], Q_BLOCK), hs], k_ref[pl.ds(blk["k0"], win), hs],
                (((1,), (1,)), ((), ())), preferred_element_type=F32)

        row_max = []
        for u, (e, h) in enumerate(units):
            s = s_scr[u, :, :win] + bias_ref[blocks[e]["band"], :, :win]
            m = jnp.max(s, axis=-1, keepdims=True)
            p_scr[u, :, :win] = jnp.exp2(s - m).astype(BF16)
            row_max.append(m)

        for u, (e, h) in enumerate(units):
            blk = blocks[e]
            hs = slice(h * HEAD_DIM, (h + 1) * HEAD_DIM)
            v_ext = jnp.concatenate([v_ref[pl.ds(blk["k0"], win), hs], ones], axis=1)
            oe = jnp.dot(p_scr[u, :, :win], v_ext, preferred_element_type=F32)
            new = (oe[:, :HEAD_DIM], oe[:, HEAD_DIM:],
                   jnp.broadcast_to(row_max[u], (Q_BLOCK, HEAD_DIM)))
            if d == st:
                rows = pl.ds(blk["q0"], Q_BLOCK)
                for slab, val in zip(slabs, new):
                    slab[h, rows, :] = val
            elif d == st * st:
                r1, r2 = blk["slot"] // st, blk["slot"] % st
                rows = pl.ds(r1 * (seq // st) + blk["q_off"] * st + r2, Q_BLOCK, stride=st)
                merged = _merge([slab[h, rows, :] for slab in slabs], new)
                for slab, val in zip(slabs, merged):
                    slab[h, rows, :] = val
            else:
                cnt = Q_BLOCK // st
                old = []
                for k, slab in enumerate(slabs):
                    t = len(slabs) * (u % TMP_UNITS) + k
                    for r in range(st):
                        src = pl.multiple_of(r * (seq // st) + blk["q0"] // st, cnt)
                        tmp_ref[t, pl.ds(r, cnt, stride=st), :] = slab[h, pl.ds(src, cnt), :]
                    old.append(tmp_ref[t])
                acc, den, _ = _merge(old, new)
                y_ref[pl.ds(blk["q0"], Q_BLOCK), hs] = (acc / den).astype(BF16)
        return carry

    lax.fori_loop(0, d * n_blk // bps, body, 0)


def _attn_kernel(q_ref, k_ref, v_ref, y_ref, acc_slab, den_slab, max_slab, tmp_ref,
                 bias_ref, s_scr, p_scr, *, seq):
    g = pl.program_id(1)

    @pl.when((pl.program_id(0) == 0) & (g == 0))
    def _band_bias():
        row = lax.broadcasted_iota(jnp.int32, (Q_BLOCK, 2 * Q_BLOCK), 0)
        col = lax.broadcasted_iota(jnp.int32, (Q_BLOCK, 2 * Q_BLOCK), 1)
        for band in range(N_BANDS):
            valid = jnp.abs(row - col + band * RADIUS) <= RADIUS
            bias_ref[band] = jnp.where(valid, 0.0, MASK_VALUE).astype(F32)

    for step, gi in enumerate(GROUP_ORDER):
        @pl.when(g == step)
        def _group(d=ATTN_PATTERNS[gi][1]):
            _attn_group(q_ref, k_ref, v_ref, y_ref, acc_slab, den_slab, max_slab, tmp_ref,
                        bias_ref, s_scr, p_scr, d, seq)


def _attn_call(proj):
    _, bsz, seq, _ = proj.shape
    n_slabs = HEADS_PER_GROUP
    n_units = BLOCKS_PER_STEP * HEADS_PER_GROUP
    assert all(GROUP_ORDER[s] == (s + GROUP_ORDER[0]) % N_GROUPS for s in range(N_GROUPS))

    def spec(piece):
        return pl.BlockSpec(
            (None, None, seq, PIECE),
            lambda b, g: (TILE_QKV + (g + GROUP_ORDER[0]) % N_GROUPS, b, 0, piece))

    return pl.pallas_call(
        functools.partial(_attn_kernel, seq=seq),
        name="attn",
        grid=(bsz, N_GROUPS),
        in_specs=[spec(0), spec(1), spec(2)],
        out_specs=pl.BlockSpec((None, seq, GROUP_WIDTH), lambda b, g: (b, 0, 0)),
        out_shape=jax.ShapeDtypeStruct((bsz, seq, GROUP_WIDTH), BF16),
        scratch_shapes=[
            pltpu.VMEM((n_slabs, seq, LANES), F32),
            pltpu.VMEM((n_slabs, seq, LANES), F32),
            pltpu.VMEM((n_slabs, seq, LANES), F32),
            pltpu.VMEM((3 * TMP_UNITS, Q_BLOCK, LANES), F32),
            pltpu.VMEM((N_BANDS, Q_BLOCK, 2 * Q_BLOCK), F32),
            pltpu.VMEM((n_units, Q_BLOCK, 2 * Q_BLOCK), F32),
            pltpu.VMEM((n_units, Q_BLOCK, 2 * Q_BLOCK), BF16),
        ],
        compiler_params=pltpu.CompilerParams(
            dimension_semantics=("arbitrary", "arbitrary"),
            vmem_limit_bytes=VMEM_LIMIT_BYTES),
    )(proj, proj, proj)


def _mix_kernel(zu_ref, zv_ref, ga_ref, gb_ref, yb_ref, x_ref,
                ws_ref, bsp_ref, wa_ref, wb_ref, wo_ref, gain_ref, o_ref, ya_ref):
    group_dim = GMLP_WIDTH // GMLP_GROUPS
    for c in range(0, MIX_ROWS // CHUNK, 2):
        rows0 = slice(c * CHUNK, (c + 1) * CHUNK)
        rows1 = slice((c + 1) * CHUNK, (c + 2) * CHUNK)
        sv = []
        for g in range(GMLP_GROUPS):
            cols = slice(g * group_dim, (g + 1) * group_dim)
            pair = jnp.concatenate([zv_ref[rows0, cols], zv_ref[rows1, cols]], axis=1)
            sv.append(jnp.dot(ws_ref[g], pair, preferred_element_type=F32))
        for k, rows in enumerate((rows0, rows1)):
            sv_k = jnp.concatenate(
                [s[:, k * group_dim:(k + 1) * group_dim] for s in sv], axis=1)
            ya_ref[rows, :] = (zu_ref[rows, :].astype(F32) * (sv_k + bsp_ref[...])).astype(BF16)
    a = jnp.dot(ya_ref[...], wa_ref[...], preferred_element_type=F32)
    b = jnp.dot(yb_ref[...], wb_ref[...], preferred_element_type=F32)
    merged = ga_ref[...].astype(F32) * a + gb_ref[...].astype(F32) * b
    mix = jnp.dot(merged.astype(BF16), wo_ref[...], preferred_element_type=F32)
    ms = jnp.mean(mix * mix, axis=-1, keepdims=True)
    o_ref[...] = x_ref[...] + mix * lax.rsqrt(ms + NORM_EPS) * gain_ref[...]


def _mix_call(proj, y_b, x, w_spatial, b_sp, w_a, w_b, w_out, gain):
    bsz, seq, d_model = x.shape

    def piece(tile, first_piece, n_pieces):
        width = n_pieces * PIECE
        return pl.BlockSpec((None, None, MIX_ROWS, width),
                            lambda b, i: (tile, b, i, first_piece // n_pieces))

    def whole(a):
        return pl.BlockSpec(a.shape, lambda b, i: (0,) * a.ndim)

    return pl.pallas_call(
        _mix_kernel,
        name="mix",
        grid=(bsz, seq // MIX_ROWS),
        in_specs=[
            piece(TILE_A, 2, 1), piece(TILE_B, 2, 1), piece(TILE_A, 0, 2), piece(TILE_B, 0, 2),
            pl.BlockSpec((None, MIX_ROWS, GROUP_WIDTH), lambda b, i: (b, i, 0)),
            pl.BlockSpec((None, MIX_ROWS, d_model), lambda b, i: (b, i, 0)),
            whole(w_spatial), whole(b_sp), whole(w_a), whole(w_b), whole(w_out), whole(gain),
        ],
        out_specs=pl.BlockSpec((None, MIX_ROWS, d_model), lambda b, i: (b, i, 0)),
        out_shape=jax.ShapeDtypeStruct((bsz, seq, d_model), F32),
        scratch_shapes=[pltpu.VMEM((MIX_ROWS, GMLP_WIDTH), BF16)],
        compiler_params=pltpu.CompilerParams(
            dimension_semantics=("parallel", "parallel"),
            vmem_limit_bytes=VMEM_LIMIT_BYTES),
    )(proj, proj, proj, proj, y_b, x, w_spatial, b_sp, w_a, w_b, w_out, gain)


def _mlp_kernel(h_ref, gpre_ref, wu_ref, wd_ref, gpost_ref, o_ref, n_ref, acc_ref, *, d_ff):
    h = h_ref[...]
    ms = jnp.mean(h * h, axis=-1, keepdims=True)
    n_ref[...] = (h * lax.rsqrt(ms + NORM_EPS) * gpre_ref[...]).astype(BF16)
    for c in range(d_ff // MLP_FF_CHUNK):
        cols = slice(c * MLP_FF_CHUNK, (c + 1) * MLP_FF_CHUNK)
        up = jnp.dot(n_ref[...], wu_ref[:, cols], preferred_element_type=F32)
        hid = jnp.square(jnp.maximum(up, 0.0)).astype(BF16)
        part = jnp.dot(hid, wd_ref[cols, :], preferred_element_type=F32)
        if c == 0:
            acc_ref[...] = part
        else:
            acc_ref[...] += part
    out = acc_ref[...]
    ms2 = jnp.mean(out * out, axis=-1, keepdims=True)
    o_ref[...] = h_ref[...] + out * lax.rsqrt(ms2 + NORM_EPS) * gpost_ref[...]


def _mlp_call(h, gain_pre, w_up, w_down, gain_post):
    bsz, seq, d_model = h.shape
    d_ff = w_up.shape[1]
    rows = bsz * seq
    h2 = h.reshape(rows, d_model)

    def whole(a):
        return pl.BlockSpec(a.shape, lambda i: (0,) * a.ndim, pipeline_mode=pl.Buffered(1))

    out = pl.pallas_call(
        functools.partial(_mlp_kernel, d_ff=d_ff),
        name="mlp",
        grid=(rows // MLP_ROWS,),
        in_specs=[
            pl.BlockSpec((MLP_ROWS, d_model), lambda i: (i, 0)),
            whole(gain_pre), whole(w_up), whole(w_down), whole(gain_post),
        ],
        out_specs=pl.BlockSpec((MLP_ROWS, d_model), lambda i: (i, 0)),
        out_shape=jax.ShapeDtypeStruct((rows, d_model), F32),
        scratch_shapes=[
            pltpu.VMEM((MLP_ROWS, d_model), BF16),
            pltpu.VMEM((MLP_ROWS, d_model), F32),
        ],
        compiler_params=pltpu.CompilerParams(
            dimension_semantics=("parallel",),
            vmem_limit_bytes=VMEM_LIMIT_BYTES),
    )(h2, gain_pre, w_up, w_down, gain_post)
    return out.reshape(bsz, seq, d_model)


def _rope_lane_order():
    half = ROPE_DIM // 2
    fill = HEAD_DIM // 2 - half
    return (list(range(half)) + list(range(ROPE_DIM, ROPE_DIM + fill))
            + list(range(half, ROPE_DIM)) + list(range(ROPE_DIM + fill, HEAD_DIM)))


def _source_piece(i):
    n_attn = 3 * N_GROUPS
    first_gate = 2 + n_attn
    in_qkv = i >= TILE_QKV * PIECES_PER_TILE
    a, c = i // PIECES_PER_TILE, i % PIECES_PER_TILE
    head_tiles = jnp.where(c < 2, first_gate + 2 * a + c, a)
    g, section = a - TILE_QKV, c
    return jnp.where(in_qkv, 2 + N_GROUPS * section + g, head_tiles)


def _wprep_kernel(w_ref, sel_ref, o_ref):
    i = pl.program_id(0)
    is_qk = (i >= TILE_QKV * PIECES_PER_TILE) & (i % PIECES_PER_TILE < 2)

    @pl.when(is_qk)
    def _reorder():
        for h in range(HEADS_PER_GROUP):
            hs = slice(h * HEAD_DIM, (h + 1) * HEAD_DIM)
            o_ref[:, hs] = jnp.dot(w_ref[:, hs].astype(BF16), sel_ref[...],
                                   preferred_element_type=F32).astype(BF16)

    @pl.when(jnp.logical_not(is_qk))
    def _copy():
        o_ref[...] = w_ref[...].astype(BF16)


def _regroup_w_in(w_in):
    d_model, d_in = w_in.shape
    select = np.zeros((HEAD_DIM, HEAD_DIM), np.float32)
    select[_rope_lane_order(), np.arange(HEAD_DIM)] = 1.0
    return pl.pallas_call(
        _wprep_kernel,
        name="wprep",
        grid=(d_in // PIECE,),
        in_specs=[
            pl.BlockSpec((d_model, PIECE), lambda i: (0, _source_piece(i))),
            pl.BlockSpec((HEAD_DIM, HEAD_DIM), lambda i: (0, 0)),
        ],
        out_specs=pl.BlockSpec((d_model, PIECE), lambda i: (0, i)),
        out_shape=jax.ShapeDtypeStruct((d_model, d_in), BF16),
        compiler_params=pltpu.CompilerParams(
            dimension_semantics=("parallel",),
            vmem_limit_bytes=VMEM_LIMIT_BYTES),
    )(w_in, jnp.asarray(select, BF16))


def _rope_tables(seq):
    half = ROPE_DIM // 2
    fill = HEAD_DIM // 2 - half
    inv_freq = np.float32(ROPE_THETA) ** (-np.arange(0, ROPE_DIM, 2, dtype=np.float32) / ROPE_DIM)
    ang = np.arange(seq, dtype=np.float32)[:, None] * inv_freq[None, :].astype(np.float32)
    cos, sin = np.cos(ang), np.sin(ang)
    ones = np.ones((seq, fill), np.float32)
    cos_l = np.concatenate([cos, ones, cos, ones], axis=1)
    sin_l = np.concatenate([-sin, 0.0 * ones, sin, 0.0 * ones], axis=1)
    nat = np.stack([cos_l, sin_l])
    per_group = []
    for _, d in ATTN_PATTERNS:
        by_residue = nat.reshape(2, seq // d, d, HEAD_DIM).transpose(0, 2, 1, 3)
        residues = [_slot_residue(slot, d) for slot in range(d)]
        per_group.append(by_residue[:, residues].reshape(2, seq, HEAD_DIM))
    return jnp.asarray(np.stack(per_group), F32)


def _layer(h, norm_mix_pre, w_in, b_gate, ln_v_gain, ln_v_bias, w_spatial, b_spatial,
           w_branch_a, w_branch_b, w_out, norm_mix_post, norm_mlp_pre, w_up, w_down,
           norm_mlp_post):
    seq = h.shape[1]
    row = lambda v: v.reshape(1, -1).astype(F32)
    proj = _proj_call(h, row(norm_mix_pre), _regroup_w_in(w_in), row(b_gate), row(ln_v_gain),
                      row(ln_v_bias), _rope_tables(seq))
    y_b = _attn_call(proj)
    b_sp = jnp.repeat(b_spatial.T.astype(F32), GMLP_WIDTH // GMLP_GROUPS, axis=1)
    h = _mix_call(proj, y_b, h, w_spatial.astype(BF16), b_sp, w_branch_a.astype(BF16),
                  w_branch_b.astype(BF16), w_out.astype(BF16), row(norm_mix_post))
    return _mlp_call(h, row(norm_mlp_pre), w_up.astype(BF16), w_down.astype(BF16),
                     row(norm_mlp_post))


def kernel(x, norm_mix_pre, w_in, b_gate, ln_v_gain, ln_v_bias, w_spatial, b_spatial,
           w_branch_a, w_branch_b, w_out, norm_mix_post, norm_mlp_pre, w_up, w_down,
           norm_mlp_post):
    h = x
    for l in range(w_in.shape[0]):
        h = _layer(h, norm_mix_pre[l], w_in[l], b_gate[l], ln_v_gain[l], ln_v_bias[l],
                   w_spatial[l], b_spatial[l], w_branch_a[l], w_branch_b[l], w_out[l],
                   norm_mix_post[l], norm_mlp_pre[l], w_up[l], w_down[l], norm_mlp_post[l])
    return h
```

```python
import functools
import math

import jax
import jax.numpy as jnp
import numpy as np
from jax import lax
from jax.experimental import pallas as pl
from jax.experimental.pallas import tpu as pltpu

GMLP_WIDTH = 512
GMLP_GROUPS = 4
CHUNK = 128
ATTN_PATTERNS = ((128, 1), (512, 4), (2048, 16))
N_GROUPS = len(ATTN_PATTERNS)
HEADS_PER_GROUP = 4
HEAD_DIM = 128
GROUP_WIDTH = HEADS_PER_GROUP * HEAD_DIM
ROPE_DIM = HEAD_DIM // 4
ROPE_THETA = 500000.0
NORM_EPS = 1e-6
MASK_VALUE = -1e30
LOG2E = math.log2(math.e)
QK_SCALE = LOG2E / math.sqrt(HEAD_DIM)
RADIUS = 64
assert all(w // (2 * d) == RADIUS for w, d in ATTN_PATTERNS)
DEINT_STRIDE = 4
assert tuple(d for _, d in ATTN_PATTERNS) == (1, DEINT_STRIDE, DEINT_STRIDE ** 2)


def _slot_residue(slot, d):
    if d == DEINT_STRIDE ** 2:
        return DEINT_STRIDE * (slot % DEINT_STRIDE) + slot // DEINT_STRIDE
    return slot


LANES = 128
VMEM_LIMIT_BYTES = 56 * 1024 * 1024

PIECE = 512
PIECES_PER_TILE = 3
TN = PIECES_PER_TILE * PIECE
PROLOGUE_ROWS = 256
PROJ_ROWS = 256
Q_BLOCK = 128
BLOCKS_PER_STEP = 16
TMP_UNITS = 16
N_BANDS = 3
GROUP_ORDER = (1, 2, 0)
MIX_ROWS = 1024
MLP_ROWS = 1024
MLP_FF_CHUNK = 1024

BF16 = jnp.bfloat16
F32 = jnp.float32

TILE_A, TILE_B, TILE_QKV = 0, 1, 2
N_TILES = TILE_QKV + N_GROUPS


def _gelu_tanh(x):
    c = math.sqrt(2.0 / math.pi)
    return x * (0.5 * (1.0 + jnp.tanh(c * (x + 0.044715 * (x * x * x)))))


def _sigmoid(x):
    return 1.0 / (1.0 + jnp.exp(-x))


def _proj_kernel(x_ref, gain_ref, w_ref, bg_ref, lng_ref, lnb_ref, rope_ref, o_ref,
                 lhs_ref, slab_ref, *, seq, d_model):
    j = pl.program_id(1)
    n_slabs = d_model // LANES

    @pl.when(j == 0)
    def _prologue():
        st = DEINT_STRIDE
        cnt1 = PROLOGUE_ROWS // st
        cnt2 = cnt1 // st

        def body(c, carry):
            r0 = pl.multiple_of(c * PROLOGUE_ROWS, PROLOGUE_ROWS)
            xc = x_ref[pl.ds(r0, PROLOGUE_ROWS), :]
            ms = jnp.mean(xc * xc, axis=-1, keepdims=True)
            n = xc * lax.rsqrt(ms + NORM_EPS) * gain_ref[...]
            lhs_ref[0, pl.ds(r0, PROLOGUE_ROWS), :] = n.astype(BF16)
            for s in range(n_slabs):
                lanes = slice(s * LANES, (s + 1) * LANES)
                slab_ref[0, s] = n[:, lanes]
                for r1 in range(st):
                    piece = slab_ref[0, s, pl.ds(r1, cnt1, stride=st), :]
                    dst1 = pl.multiple_of(r1 * (seq // st) + c * cnt1, cnt1)
                    lhs_ref[1, pl.ds(dst1, cnt1), lanes] = piece.astype(BF16)
                    slab_ref[1, s, r1 * cnt1:(r1 + 1) * cnt1, :] = piece
                for r1 in range(st):
                    for r2 in range(st):
                        sub = slab_ref[1, s, pl.ds(r1 * cnt1 + r2, cnt2, stride=st), :]
                        slot = r1 * st + r2
                        dst2 = pl.multiple_of(slot * (seq // (st * st)) + c * cnt2, cnt2)
                        lhs_ref[2, pl.ds(dst2, cnt2), lanes] = sub.astype(BF16)
            return carry

        lax.fori_loop(0, seq // PROLOGUE_ROWS, body, 0)

    def matmul_rows(lhs_idx, epilogues):
        for c in range(seq // PROJ_ROWS):
            r0 = c * PROJ_ROWS
            acc = jnp.dot(lhs_ref[lhs_idx, pl.ds(r0, PROJ_ROWS), :], w_ref[...],
                          preferred_element_type=F32)
            for c0, width, fn in epilogues:
                o_ref[pl.ds(r0, PROJ_ROWS), c0:c0 + width] = (
                    fn(acc[:, c0:c0 + width], r0).astype(BF16))

    def gates(acc, r0):
        return _sigmoid(acc + bg_ref[...])

    def layer_norm_gelu(acc, r0):
        z = _gelu_tanh(acc)
        mu = jnp.mean(z, axis=-1, keepdims=True)
        zc = z - mu
        var = jnp.mean(zc * zc, axis=-1, keepdims=True)
        return zc * lax.rsqrt(var + NORM_EPS) * lng_ref[...] + lnb_ref[...]

    def rotary(scale):
        def fn(acc, r0):
            cos = rope_ref[0, pl.ds(r0, PROJ_ROWS), :]
            sin = rope_ref[1, pl.ds(r0, PROJ_ROWS), :]
            heads = []
            for h in range(HEADS_PER_GROUP):
                t = acc[:, h * HEAD_DIM:(h + 1) * HEAD_DIM]
                out = t * cos + pltpu.roll(t, HEAD_DIM // 2, 1) * sin
                heads.append(out if scale is None else out * scale)
            return jnp.concatenate(heads, axis=1)

        return fn

    @pl.when(j == TILE_A)
    def _tile_a():
        matmul_rows(0, [(0, 2 * PIECE, gates),
                        (2 * PIECE, PIECE, lambda acc, r0: _gelu_tanh(acc))])

    @pl.when(j == TILE_B)
    def _tile_b():
        matmul_rows(0, [(0, 2 * PIECE, gates), (2 * PIECE, PIECE, layer_norm_gelu)])

    @pl.when(j >= TILE_QKV)
    def _tile_qkv():
        matmul_rows(j - TILE_QKV, [(0, PIECE, rotary(QK_SCALE)),
                                   (PIECE, PIECE, rotary(None)),
                                   (2 * PIECE, PIECE, lambda acc, r0: acc)])


def _proj_call(x, gain, w_in, b_gate, ln_gain, ln_bias, rope_tab):
    bsz, seq, d_model = x.shape
    assert w_in.shape[1] == N_TILES * TN
    kern = functools.partial(_proj_kernel, seq=seq, d_model=d_model)
    return pl.pallas_call(
        kern,
        name="proj",
        grid=(bsz, N_TILES),
        in_specs=[
            pl.BlockSpec((None, seq, d_model), lambda b, j: (b, 0, 0)),
            pl.BlockSpec((1, d_model), lambda b, j: (0, 0)),
            pl.BlockSpec((d_model, TN), lambda b, j: (0, j)),
            pl.BlockSpec((1, 2 * PIECE), lambda b, j: (0, jnp.minimum(j, TILE_B))),
            pl.BlockSpec((1, PIECE), lambda b, j: (0, 0)),
            pl.BlockSpec((1, PIECE), lambda b, j: (0, 0)),
            pl.BlockSpec((None, 2, seq, HEAD_DIM),
                         lambda b, j: (jnp.maximum(j - TILE_QKV, 0), 0, 0, 0)),
        ],
        out_specs=pl.BlockSpec((None, None, seq, TN), lambda b, j: (j, b, 0, 0)),
        out_shape=jax.ShapeDtypeStruct((N_TILES, bsz, seq, TN), BF16),
        scratch_shapes=[
            pltpu.VMEM((N_GROUPS, seq, d_model), BF16),
            pltpu.VMEM((2, d_model // LANES, PROLOGUE_ROWS, LANES), F32),
        ],
        compiler_params=pltpu.CompilerParams(
            dimension_semantics=("arbitrary", "arbitrary"),
            vmem_limit_bytes=VMEM_LIMIT_BYTES),
    )(x, gain, w_in, b_gate, ln_gain, ln_bias, rope_tab)


def _pack_stats(den, row_max):
    low = lax.broadcasted_iota(jnp.int32, den.shape, 1) < HEAD_DIM // 2
    return jnp.where(low, row_max, den)


def _unpack_stats(stats):
    low = lax.broadcasted_iota(jnp.int32, stats.shape, 1) < HEAD_DIM // 2
    other = pltpu.roll(stats, HEAD_DIM // 2, 1)
    return jnp.where(low, other, stats), jnp.where(low, stats, other)


def _merge(old, new):
    acc0, den0, max0 = old
    acc1, den1, max1 = new
    m = jnp.maximum(max0, max1)
    w0 = jnp.exp2(max0 - m)
    w1 = jnp.exp2(max1 - m)
    return acc0 * w0 + acc1 * w1, den0 * w0 + den1 * w1, m


def _attn_group(q_ref, k_ref, v_ref, y_ref, acc_slab, stat_slab, tmp_ref,
                bias_ref, s_scr, p_scr, d, seq):
    st = DEINT_STRIDE
    sub_len = seq // d
    n_blk = sub_len // Q_BLOCK
    win = min(2 * Q_BLOCK, sub_len)
    ones = jnp.ones((win, HEAD_DIM), BF16)
    slabs = (acc_slab, stat_slab)
    bps = BLOCKS_PER_STEP

    def body(step, carry):
        blocks = []
        for e in range(bps):
            f = step * bps + e
            slot = f // n_blk
            q_off = (f % n_blk) * Q_BLOCK
            k_off = jnp.clip(q_off - RADIUS, 0, sub_len - win)
            blocks.append(dict(
                slot=slot, q_off=q_off, band=(q_off - k_off) // RADIUS,
                q0=pl.multiple_of(slot * sub_len + q_off, Q_BLOCK),
                k0=pl.multiple_of(slot * sub_len + k_off, RADIUS)))
        units = [(e, h) for e in range(bps) for h in range(HEADS_PER_GROUP)]

        for u, (e, h) in enumerate(units):
            blk = blocks[e]
            hs = slice(h * HEAD_DIM, (h + 1) * HEAD_DIM)
            s_scr[u, :, :win] = lax.dot_general(
                q_ref[pl.ds(blk["q0"], Q_BLOCK), hs], k_ref[pl.ds(blk["k0"], win), hs],
                (((1,), (1,)), ((), ())), preferred_element_type=F32)

        row_max = []
        for u, (e, h) in enumerate(units):
            s = s_scr[u, :, :win] + bias_ref[blocks[e]["band"], :, :win]
            m = jnp.max(s, axis=-1, keepdims=True)
            p_scr[u, :, :win] = jnp.exp2(s - m).astype(BF16)
            row_max.append(m)

        for u, (e, h) in enumerate(units):
            blk = blocks[e]
            hs = slice(h * HEAD_DIM, (h + 1) * HEAD_DIM)
            v_ext = jnp.concatenate([v_ref[pl.ds(blk["k0"], win), hs], ones], axis=1)
            oe = jnp.dot(p_scr[u, :, :win], v_ext, preferred_element_type=F32)
            new = (oe[:, :HEAD_DIM], oe[:, HEAD_DIM:],
                   jnp.broadcast_to(row_max[u], (Q_BLOCK, HEAD_DIM)))
            if d == st:
                rows = pl.ds(blk["q0"], Q_BLOCK)
                acc_slab[h, rows, :] = new[0]
                stat_slab[h, rows, :] = _pack_stats(new[1], new[2])
            elif d == st * st:
                r1, r2 = blk["slot"] // st, blk["slot"] % st
                rows = pl.ds(r1 * (seq // st) + blk["q_off"] * st + r2, Q_BLOCK, stride=st)
                old = (acc_slab[h, rows, :],) + _unpack_stats(stat_slab[h, rows, :])
                acc, den, m = _merge(old, new)
                acc_slab[h, rows, :] = acc
                stat_slab[h, rows, :] = _pack_stats(den, m)
            else:
                cnt = Q_BLOCK // st
                gathered = []
                for k, slab in enumerate(slabs):
                    t = len(slabs) * (u % TMP_UNITS) + k
                    for r in range(st):
                        src = pl.multiple_of(r * (seq // st) + blk["q0"] // st, cnt)
                        tmp_ref[t, pl.ds(r, cnt, stride=st), :] = slab[h, pl.ds(src, cnt), :]
                    gathered.append(tmp_ref[t])
                old = (gathered[0],) + _unpack_stats(gathered[1])
                acc, den, _ = _merge(old, new)
                y_ref[pl.ds(blk["q0"], Q_BLOCK), hs] = (acc / den).astype(BF16)
        return carry

    lax.fori_loop(0, d * n_blk // bps, body, 0)


def _attn_kernel(q_ref, k_ref, v_ref, y_ref, acc_slab, stat_slab, tmp_ref,
                 bias_ref, s_scr, p_scr, *, seq):
    g = pl.program_id(1)

    @pl.when((pl.program_id(0) == 0) & (g == 0))
    def _band_bias():
        row = lax.broadcasted_iota(jnp.int32, (Q_BLOCK, 2 * Q_BLOCK), 0)
        col = lax.broadcasted_iota(jnp.int32, (Q_BLOCK, 2 * Q_BLOCK), 1)
        for band in range(N_BANDS):
            valid = jnp.abs(row - col + band * RADIUS) <= RADIUS
            bias_ref[band] = jnp.where(valid, 0.0, MASK_VALUE).astype(F32)

    for step, gi in enumerate(GROUP_ORDER):
        @pl.when(g == step)
        def _group(d=ATTN_PATTERNS[gi][1]):
            _attn_group(q_ref, k_ref, v_ref, y_ref, acc_slab, stat_slab, tmp_ref,
                        bias_ref, s_scr, p_scr, d, seq)


def _attn_call(proj):
    _, bsz, seq, _ = proj.shape
    n_slabs = HEADS_PER_GROUP
    n_units = BLOCKS_PER_STEP * HEADS_PER_GROUP
    assert all(GROUP_ORDER[s] == (s + GROUP_ORDER[0]) % N_GROUPS for s in range(N_GROUPS))

    def spec(piece):
        return pl.BlockSpec(
            (None, None, seq, PIECE),
            lambda b, g: (TILE_QKV + (g + GROUP_ORDER[0]) % N_GROUPS, b, 0, piece))

    return pl.pallas_call(
        functools.partial(_attn_kernel, seq=seq),
        name="attn",
        grid=(bsz, N_GROUPS),
        in_specs=[spec(0), spec(1), spec(2)],
        out_specs=pl.BlockSpec((None, seq, GROUP_WIDTH), lambda b, g: (b, 0, 0)),
        out_shape=jax.ShapeDtypeStruct((bsz, seq, GROUP_WIDTH), BF16),
        scratch_shapes=[
            pltpu.VMEM((n_slabs, seq, LANES), F32),
            pltpu.VMEM((n_slabs, seq, LANES), F32),
            pltpu.VMEM((2 * TMP_UNITS, Q_BLOCK, LANES), F32),
            pltpu.VMEM((N_BANDS, Q_BLOCK, 2 * Q_BLOCK), F32),
            pltpu.VMEM((n_units, Q_BLOCK, 2 * Q_BLOCK), F32),
            pltpu.VMEM((n_units, Q_BLOCK, 2 * Q_BLOCK), BF16),
        ],
        compiler_params=pltpu.CompilerParams(
            dimension_semantics=("arbitrary", "arbitrary"),
            vmem_limit_bytes=VMEM_LIMIT_BYTES),
    )(proj, proj, proj)


def _mix_kernel(zu_ref, zv_ref, ga_ref, gb_ref, yb_ref, x_ref,
                ws_ref, bsp_ref, wa_ref, wb_ref, wo_ref, gain_ref, o_ref, ya_ref):
    group_dim = GMLP_WIDTH // GMLP_GROUPS
    half_rows = MIX_ROWS // 2

    def spatial(half):
        first = half * half_rows // CHUNK
        for c in range(first, first + half_rows // CHUNK, 2):
            rows0 = slice(c * CHUNK, (c + 1) * CHUNK)
            rows1 = slice((c + 1) * CHUNK, (c + 2) * CHUNK)
            sv = []
            for g in range(GMLP_GROUPS):
                cols = slice(g * group_dim, (g + 1) * group_dim)
                pair = jnp.concatenate([zv_ref[rows0, cols], zv_ref[rows1, cols]], axis=1)
                sv.append(jnp.dot(ws_ref[g], pair, preferred_element_type=F32))
            for k, rows in enumerate((rows0, rows1)):
                sv_k = jnp.concatenate(
                    [s[:, k * group_dim:(k + 1) * group_dim] for s in sv], axis=1)
                ya_ref[rows, :] = (
                    zu_ref[rows, :].astype(F32) * (sv_k + bsp_ref[...])).astype(BF16)

    def branches(rows):
        a = jnp.dot(ya_ref[rows, :], wa_ref[...], preferred_element_type=F32)
        b = jnp.dot(yb_ref[rows, :], wb_ref[...], preferred_element_type=F32)
        return (ga_ref[rows, :].astype(F32) * a + gb_ref[rows, :].astype(F32) * b).astype(BF16)

    def finish(rows, mix):
        ms = jnp.mean(mix * mix, axis=-1, keepdims=True)
        o_ref[rows, :] = x_ref[rows, :] + mix * lax.rsqrt(ms + NORM_EPS) * gain_ref[...]

    lo, hi = slice(0, half_rows), slice(half_rows, MIX_ROWS)
    spatial(0)
    merged_lo = branches(lo)
    spatial(1)
    mix_lo = jnp.dot(merged_lo, wo_ref[...], preferred_element_type=F32)
    merged_hi = branches(hi)
    finish(lo, mix_lo)
    mix_hi = jnp.dot(merged_hi, wo_ref[...], preferred_element_type=F32)
    finish(hi, mix_hi)


def _mix_call(proj, y_b, x, w_spatial, b_sp, w_a, w_b, w_out, gain):
    bsz, seq, d_model = x.shape

    def piece(tile, first_piece, n_pieces):
        width = n_pieces * PIECE
        return pl.BlockSpec((None, None, MIX_ROWS, width),
                            lambda b, i: (tile, b, i, first_piece // n_pieces))

    def whole(a):
        return pl.BlockSpec(a.shape, lambda b, i: (0,) * a.ndim)

    return pl.pallas_call(
        _mix_kernel,
        name="mix",
        grid=(bsz, seq // MIX_ROWS),
        in_specs=[
            piece(TILE_A, 2, 1), piece(TILE_B, 2, 1), piece(TILE_A, 0, 2), piece(TILE_B, 0, 2),
            pl.BlockSpec((None, MIX_ROWS, GROUP_WIDTH), lambda b, i: (b, i, 0)),
            pl.BlockSpec((None, MIX_ROWS, d_model), lambda b, i: (b, i, 0)),
            whole(w_spatial), whole(b_sp), whole(w_a), whole(w_b), whole(w_out), whole(gain),
        ],
        out_specs=pl.BlockSpec((None, MIX_ROWS, d_model), lambda b, i: (b, i, 0)),
        out_shape=jax.ShapeDtypeStruct((bsz, seq, d_model), F32),
        scratch_shapes=[pltpu.VMEM((MIX_ROWS, GMLP_WIDTH), BF16)],
        compiler_params=pltpu.CompilerParams(
            dimension_semantics=("parallel", "parallel"),
            vmem_limit_bytes=VMEM_LIMIT_BYTES),
    )(proj, proj, proj, proj, y_b, x, w_spatial, b_sp, w_a, w_b, w_out, gain)


def _mlp_kernel(h_ref, gpre_ref, wu_ref, wd_ref, gpost_ref, o_ref, n_ref, acc_ref, *, d_ff):
    h = h_ref[...]
    ms = jnp.mean(h * h, axis=-1, keepdims=True)
    n_ref[...] = (h * lax.rsqrt(ms + NORM_EPS) * gpre_ref[...]).astype(BF16)
    for c in range(d_ff // MLP_FF_CHUNK):
        cols = slice(c * MLP_FF_CHUNK, (c + 1) * MLP_FF_CHUNK)
        up = jnp.dot(n_ref[...], wu_ref[:, cols], preferred_element_type=F32)
        hid = jnp.square(jnp.maximum(up, 0.0)).astype(BF16)
        part = jnp.dot(hid, wd_ref[cols, :], preferred_element_type=F32)
        if c == 0:
            acc_ref[...] = part
        else:
            acc_ref[...] += part
    out = acc_ref[...]
    ms2 = jnp.mean(out * out, axis=-1, keepdims=True)
    o_ref[...] = h_ref[...] + out * lax.rsqrt(ms2 + NORM_EPS) * gpost_ref[...]


def _mlp_call(h, gain_pre, w_up, w_down, gain_post):
    bsz, seq, d_model = h.shape
    d_ff = w_up.shape[1]
    rows = bsz * seq
    h2 = h.reshape(rows, d_model)

    def whole(a):
        return pl.BlockSpec(a.shape, lambda i: (0,) * a.ndim, pipeline_mode=pl.Buffered(1))

    out = pl.pallas_call(
        functools.partial(_mlp_kernel, d_ff=d_ff),
        name="mlp",
        grid=(rows // MLP_ROWS,),
        in_specs=[
            pl.BlockSpec((MLP_ROWS, d_model), lambda i: (i, 0)),
            whole(gain_pre), whole(w_up), whole(w_down), whole(gain_post),
        ],
        out_specs=pl.BlockSpec((MLP_ROWS, d_model), lambda i: (i, 0)),
        out_shape=jax.ShapeDtypeStruct((rows, d_model), F32),
        scratch_shapes=[
            pltpu.VMEM((MLP_ROWS, d_model), BF16),
            pltpu.VMEM((MLP_ROWS, d_model), F32),
        ],
        compiler_params=pltpu.CompilerParams(
            dimension_semantics=("parallel",),
            vmem_limit_bytes=VMEM_LIMIT_BYTES),
    )(h2, gain_pre, w_up, w_down, gain_post)
    return out.reshape(bsz, seq, d_model)


def _rope_lane_order():
    half = ROPE_DIM // 2
    fill = HEAD_DIM // 2 - half
    return (list(range(half)) + list(range(ROPE_DIM, ROPE_DIM + fill))
            + list(range(half, ROPE_DIM)) + list(range(ROPE_DIM + fill, HEAD_DIM)))


def _source_piece(i):
    n_attn = 3 * N_GROUPS
    first_gate = 2 + n_attn
    in_qkv = i >= TILE_QKV * PIECES_PER_TILE
    a, c = i // PIECES_PER_TILE, i % PIECES_PER_TILE
    head_tiles = jnp.where(c < 2, first_gate + 2 * a + c, a)
    g, section = a - TILE_QKV, c
    return jnp.where(in_qkv, 2 + N_GROUPS * section + g, head_tiles)


def _wprep_kernel(w_ref, sel_ref, o_ref):
    i = pl.program_id(0)
    is_qk = (i >= TILE_QKV * PIECES_PER_TILE) & (i % PIECES_PER_TILE < 2)

    @pl.when(is_qk)
    def _reorder():
        for h in range(HEADS_PER_GROUP):
            hs = slice(h * HEAD_DIM, (h + 1) * HEAD_DIM)
            o_ref[:, hs] = jnp.dot(w_ref[:, hs].astype(BF16), sel_ref[...],
                                   preferred_element_type=F32).astype(BF16)

    @pl.when(jnp.logical_not(is_qk))
    def _copy():
        o_ref[...] = w_ref[...].astype(BF16)


def _regroup_w_in(w_in):
    d_model, d_in = w_in.shape
    select = np.zeros((HEAD_DIM, HEAD_DIM), np.float32)
    select[_rope_lane_order(), np.arange(HEAD_DIM)] = 1.0
    return pl.pallas_call(
        _wprep_kernel,
        name="wprep",
        grid=(d_in // PIECE,),
        in_specs=[
            pl.BlockSpec((d_model, PIECE), lambda i: (0, _source_piece(i))),
            pl.BlockSpec((HEAD_DIM, HEAD_DIM), lambda i: (0, 0)),
        ],
        out_specs=pl.BlockSpec((d_model, PIECE), lambda i: (0, i)),
        out_shape=jax.ShapeDtypeStruct((d_model, d_in), BF16),
        compiler_params=pltpu.CompilerParams(
            dimension_semantics=("parallel",),
            vmem_limit_bytes=VMEM_LIMIT_BYTES),
    )(w_in, jnp.asarray(select, BF16))


def _rope_tables(seq):
    half = ROPE_DIM // 2
    fill = HEAD_DIM // 2 - half
    inv_freq = np.float32(ROPE_THETA) ** (-np.arange(0, ROPE_DIM, 2, dtype=np.float32) / ROPE_DIM)
    ang = np.arange(seq, dtype=np.float32)[:, None] * inv_freq[None, :].astype(np.float32)
    cos, sin = np.cos(ang), np.sin(ang)
    ones = np.ones((seq, fill), np.float32)
    cos_l = np.concatenate([cos, ones, cos, ones], axis=1)
    sin_l = np.concatenate([-sin, 0.0 * ones, sin, 0.0 * ones], axis=1)
    nat = np.stack([cos_l, sin_l])
    per_group = []
    for _, d in ATTN_PATTERNS:
        by_residue = nat.reshape(2, seq // d, d, HEAD_DIM).transpose(0, 2, 1, 3)
        residues = [_slot_residue(slot, d) for slot in range(d)]
        per_group.append(by_residue[:, residues].reshape(2, seq, HEAD_DIM))
    return jnp.asarray(np.stack(per_group), F32)


def _layer(h, norm_mix_pre, w_in, b_gate, ln_v_gain, ln_v_bias, w_spatial, b_spatial,
           w_branch_a, w_branch_b, w_out, norm_mix_post, norm_mlp_pre, w_up, w_down,
           norm_mlp_post):
    seq = h.shape[1]
    row = lambda v: v.reshape(1, -1).astype(F32)
    proj = _proj_call(h, row(norm_mix_pre), _regroup_w_in(w_in), row(b_gate), row(ln_v_gain),
                      row(ln_v_bias), _rope_tables(seq))
    y_b = _attn_call(proj)
    b_sp = jnp.repeat(b_spatial.T.astype(F32), GMLP_WIDTH // GMLP_GROUPS, axis=1)
    h = _mix_call(proj, y_b, h, w_spatial.astype(BF16), b_sp, w_branch_a.astype(BF16),
                  w_branch_b.astype(BF16), w_out.astype(BF16), row(norm_mix_post))
    return _mlp_call(h, row(norm_mlp_pre), w_up.astype(BF16), w_down.astype(BF16),
                     row(norm_mlp_post))


def kernel(x, norm_mix_pre, w_in, b_gate, ln_v_gain, ln_v_bias, w_spatial, b_spatial,
           w_branch_a, w_branch_b, w_out, norm_mix_post, norm_mlp_pre, w_up, w_down,
           norm_mlp_post):
    h = x
    for l in range(w_in.shape[0]):
        h = _layer(h, norm_mix_pre[l], w_in[l], b_gate[l], ln_v_gain[l], ln_v_bias[l],
                   w_spatial[l], b_spatial[l], w_branch_a[l], w_branch_b[l], w_out[l],
                   norm_mix_post[l], norm_mlp_pre[l], w_up[l], w_down[l], norm_mlp_post[l])
    return h
```

```python
import functools
import math

import jax
import jax.numpy as jnp
import numpy as np
from jax import lax
from jax.experimental import pallas as pl
from jax.experimental.pallas import tpu as pltpu

GMLP_WIDTH = 512
GMLP_GROUPS = 4
CHUNK = 128
ATTN_PATTERNS = ((128, 1), (512, 4), (2048, 16))
N_GROUPS = len(ATTN_PATTERNS)
HEADS_PER_GROUP = 4
HEAD_DIM = 128
GROUP_WIDTH = HEADS_PER_GROUP * HEAD_DIM
ROPE_DIM = HEAD_DIM // 4
ROPE_THETA = 500000.0
NORM_EPS = 1e-6
MASK_VALUE = -1e30
LOG2E = math.log2(math.e)
QK_SCALE = LOG2E / math.sqrt(HEAD_DIM)
RADIUS = 64
assert all(w // (2 * d) == RADIUS for w, d in ATTN_PATTERNS)
DEINT_STRIDE = 4
assert tuple(d for _, d in ATTN_PATTERNS) == (1, DEINT_STRIDE, DEINT_STRIDE ** 2)


def _slot_residue(slot, d):
    if d == DEINT_STRIDE ** 2:
        return DEINT_STRIDE * (slot % DEINT_STRIDE) + slot // DEINT_STRIDE
    return slot


LANES = 128
VMEM_LIMIT_BYTES = 56 * 1024 * 1024

PIECE = 512
PIECES_PER_TILE = 3
TN = PIECES_PER_TILE * PIECE
PROLOGUE_ROWS = 256
PROJ_ROWS = 256
Q_BLOCK = 128
BLOCKS_PER_STEP = 16
TMP_UNITS = 16
N_BANDS = 3
GROUP_ORDER = (1, 2, 0)
MIX_ROWS = 1024
MLP_ROWS = 1024
MLP_FF_CHUNK = 1024

BF16 = jnp.bfloat16
F32 = jnp.float32

TILE_A, TILE_B, TILE_QKV = 0, 1, 2
N_TILES = TILE_QKV + N_GROUPS


def _gelu_tanh(x):
    c = math.sqrt(2.0 / math.pi)
    return x * (0.5 * (1.0 + jnp.tanh(c * (x + 0.044715 * (x * x * x)))))


def _sigmoid(x):
    return 1.0 / (1.0 + jnp.exp(-x))


def _proj_kernel(x_ref, gain_ref, w_ref, bg_ref, lng_ref, lnb_ref, rope_ref, o_ref,
                 lhs_ref, slab_ref, *, seq, d_model):
    j = pl.program_id(1)
    n_slabs = d_model // LANES

    @pl.when(j == 0)
    def _prologue():
        st = DEINT_STRIDE
        cnt1 = PROLOGUE_ROWS // st
        cnt2 = cnt1 // st

        def body(c, carry):
            r0 = pl.multiple_of(c * PROLOGUE_ROWS, PROLOGUE_ROWS)
            xc = x_ref[pl.ds(r0, PROLOGUE_ROWS), :]
            ms = jnp.mean(xc * xc, axis=-1, keepdims=True)
            n = xc * lax.rsqrt(ms + NORM_EPS) * gain_ref[...]
            lhs_ref[0, pl.ds(r0, PROLOGUE_ROWS), :] = n.astype(BF16)
            for s in range(n_slabs):
                lanes = slice(s * LANES, (s + 1) * LANES)
                slab_ref[0, s] = n[:, lanes]
                for r1 in range(st):
                    piece = slab_ref[0, s, pl.ds(r1, cnt1, stride=st), :]
                    dst1 = pl.multiple_of(r1 * (seq // st) + c * cnt1, cnt1)
                    lhs_ref[1, pl.ds(dst1, cnt1), lanes] = piece.astype(BF16)
                    slab_ref[1, s, r1 * cnt1:(r1 + 1) * cnt1, :] = piece
                for r1 in range(st):
                    for r2 in range(st):
                        sub = slab_ref[1, s, pl.ds(r1 * cnt1 + r2, cnt2, stride=st), :]
                        slot = r1 * st + r2
                        dst2 = pl.multiple_of(slot * (seq // (st * st)) + c * cnt2, cnt2)
                        lhs_ref[2, pl.ds(dst2, cnt2), lanes] = sub.astype(BF16)
            return carry

        lax.fori_loop(0, seq // PROLOGUE_ROWS, body, 0)

    def matmul_rows(lhs_idx, epilogues):
        for c in range(seq // PROJ_ROWS):
            r0 = c * PROJ_ROWS
            acc = jnp.dot(lhs_ref[lhs_idx, pl.ds(r0, PROJ_ROWS), :], w_ref[...],
                          preferred_element_type=F32)
            for c0, width, fn in epilogues:
                o_ref[pl.ds(r0, PROJ_ROWS), c0:c0 + width] = (
                    fn(acc[:, c0:c0 + width], r0).astype(BF16))

    def gates(acc, r0):
        return _sigmoid(acc + bg_ref[...])

    def layer_norm_gelu(acc, r0):
        z = _gelu_tanh(acc)
        mu = jnp.mean(z, axis=-1, keepdims=True)
        zc = z - mu
        var = jnp.mean(zc * zc, axis=-1, keepdims=True)
        return zc * lax.rsqrt(var + NORM_EPS) * lng_ref[...] + lnb_ref[...]

    def rotary(scale):
        def fn(acc, r0):
            cos = rope_ref[0, pl.ds(r0, PROJ_ROWS), :]
            sin = rope_ref[1, pl.ds(r0, PROJ_ROWS), :]
            heads = []
            for h in range(HEADS_PER_GROUP):
                t = acc[:, h * HEAD_DIM:(h + 1) * HEAD_DIM]
                out = t * cos + pltpu.roll(t, HEAD_DIM // 2, 1) * sin
                heads.append(out if scale is None else out * scale)
            return jnp.concatenate(heads, axis=1)

        return fn

    @pl.when(j == TILE_A)
    def _tile_a():
        matmul_rows(0, [(0, 2 * PIECE, gates),
                        (2 * PIECE, PIECE, lambda acc, r0: _gelu_tanh(acc))])

    @pl.when(j == TILE_B)
    def _tile_b():
        matmul_rows(0, [(0, 2 * PIECE, gates), (2 * PIECE, PIECE, layer_norm_gelu)])

    @pl.when(j >= TILE_QKV)
    def _tile_qkv():
        matmul_rows(j - TILE_QKV, [(0, PIECE, rotary(QK_SCALE)),
                                   (PIECE, PIECE, rotary(None)),
                                   (2 * PIECE, PIECE, lambda acc, r0: acc)])


def _proj_call(x, gain, w_in, b_gate, ln_gain, ln_bias, rope_tab):
    bsz, seq, d_model = x.shape
    assert w_in.shape[1] == N_TILES * TN
    kern = functools.partial(_proj_kernel, seq=seq, d_model=d_model)
    return pl.pallas_call(
        kern,
        name="proj",
        grid=(bsz, N_TILES),
        in_specs=[
            pl.BlockSpec((None, seq, d_model), lambda b, j: (b, 0, 0)),
            pl.BlockSpec((1, d_model), lambda b, j: (0, 0)),
            pl.BlockSpec((d_model, TN), lambda b, j: (0, j)),
            pl.BlockSpec((1, 2 * PIECE), lambda b, j: (0, jnp.minimum(j, TILE_B))),
            pl.BlockSpec((1, PIECE), lambda b, j: (0, 0)),
            pl.BlockSpec((1, PIECE), lambda b, j: (0, 0)),
            pl.BlockSpec((None, 2, seq, HEAD_DIM),
                         lambda b, j: (jnp.maximum(j - TILE_QKV, 0), 0, 0, 0)),
        ],
        out_specs=pl.BlockSpec((None, None, seq, TN), lambda b, j: (j, b, 0, 0)),
        out_shape=jax.ShapeDtypeStruct((N_TILES, bsz, seq, TN), BF16),
        scratch_shapes=[
            pltpu.VMEM((N_GROUPS, seq, d_model), BF16),
            pltpu.VMEM((2, d_model // LANES, PROLOGUE_ROWS, LANES), F32),
        ],
        compiler_params=pltpu.CompilerParams(
            dimension_semantics=("arbitrary", "arbitrary"),
            vmem_limit_bytes=VMEM_LIMIT_BYTES),
    )(x, gain, w_in, b_gate, ln_gain, ln_bias, rope_tab)


def _merge(old, new):
    acc0, den0, max0 = old
    acc1, den1, max1 = new
    m = jnp.maximum(max0, max1)
    w0 = jnp.exp2(max0 - m)
    w1 = jnp.exp2(max1 - m)
    return acc0 * w0 + acc1 * w1, den0 * w0 + den1 * w1, m


def _attn_group(q_ref, k_ref, v_ref, y_ref, acc_slab, den_slab, max_slab, tmp_ref,
                band_ref, s_scr, p_scr, d, seq):
    st = DEINT_STRIDE
    sub_len = seq // d
    n_blk = sub_len // Q_BLOCK
    win = min(2 * Q_BLOCK, sub_len)
    ones = jnp.ones((win, HEAD_DIM), BF16)
    slabs = (acc_slab, den_slab, max_slab)
    bps = BLOCKS_PER_STEP

    def body(step, carry):
        blocks = []
        for e in range(bps):
            f = step * bps + e
            slot = f // n_blk
            q_off = (f % n_blk) * Q_BLOCK
            k_off = jnp.clip(q_off - RADIUS, 0, sub_len - win)
            blocks.append(dict(
                slot=slot, q_off=q_off, band=(q_off - k_off) // RADIUS,
                q0=pl.multiple_of(slot * sub_len + q_off, Q_BLOCK),
                k0=pl.multiple_of(slot * sub_len + k_off, RADIUS)))
        units = [(e, h) for e in range(bps) for h in range(HEADS_PER_GROUP)]

        for u, (e, h) in enumerate(units):
            blk = blocks[e]
            hs = slice(h * HEAD_DIM, (h + 1) * HEAD_DIM)
            s_scr[u, :, :win] = lax.dot_general(
                q_ref[pl.ds(blk["q0"], Q_BLOCK), hs], k_ref[pl.ds(blk["k0"], win), hs],
                (((1,), (1,)), ((), ())), preferred_element_type=F32)

        row_max = []
        for u, (e, h) in enumerate(units):
            s = jnp.where(band_ref[blocks[e]["band"], :, :win] > 0.0, s_scr[u, :, :win],
                          MASK_VALUE)
            m = jnp.max(s, axis=-1, keepdims=True)
            p_scr[u, :, :win] = jnp.exp2(s - m).astype(BF16)
            row_max.append(m)

        for u, (e, h) in enumerate(units):
            blk = blocks[e]
            hs = slice(h * HEAD_DIM, (h + 1) * HEAD_DIM)
            v_ext = jnp.concatenate([v_ref[pl.ds(blk["k0"], win), hs], ones], axis=1)
            oe = jnp.dot(p_scr[u, :, :win], v_ext, preferred_element_type=F32)
            new = (oe[:, :HEAD_DIM], oe[:, HEAD_DIM:],
                   jnp.broadcast_to(row_max[u], (Q_BLOCK, HEAD_DIM)))
            if d == st:
                rows = pl.ds(blk["q0"], Q_BLOCK)
                for slab, val in zip(slabs, new):
                    slab[h, rows, :] = val
            elif d == st * st:
                r1, r2 = blk["slot"] // st, blk["slot"] % st
                rows = pl.ds(r1 * (seq // st) + blk["q_off"] * st + r2, Q_BLOCK, stride=st)
                merged = _merge([slab[h, rows, :] for slab in slabs], new)
                for slab, val in zip(slabs, merged):
                    slab[h, rows, :] = val
            else:
                cnt = Q_BLOCK // st
                old = []
                for k, slab in enumerate(slabs):
                    t = len(slabs) * (u % TMP_UNITS) + k
                    for r in range(st):
                        src = pl.multiple_of(r * (seq // st) + blk["q0"] // st, cnt)
                        tmp_ref[t, pl.ds(r, cnt, stride=st), :] = slab[h, pl.ds(src, cnt), :]
                    old.append(tmp_ref[t])
                acc, den, _ = _merge(old, new)
                y_ref[pl.ds(blk["q0"], Q_BLOCK), hs] = (acc / den).astype(BF16)
        return carry

    lax.fori_loop(0, d * n_blk // bps, body, 0)


def _attn_kernel(q_ref, k_ref, v_ref, y_ref, acc_slab, den_slab, max_slab, tmp_ref,
                 band_ref, s_scr, p_scr, *, seq):
    g = pl.program_id(1)

    @pl.when((pl.program_id(0) == 0) & (g == 0))
    def _band_masks():
        row = lax.broadcasted_iota(jnp.int32, (Q_BLOCK, 2 * Q_BLOCK), 0)
        col = lax.broadcasted_iota(jnp.int32, (Q_BLOCK, 2 * Q_BLOCK), 1)
        for band in range(N_BANDS):
            valid = jnp.abs(row - col + band * RADIUS) <= RADIUS
            band_ref[band] = jnp.where(valid, 1.0, 0.0).astype(F32)

    for step, gi in enumerate(GROUP_ORDER):
        @pl.when(g == step)
        def _group(d=ATTN_PATTERNS[gi][1]):
            _attn_group(q_ref, k_ref, v_ref, y_ref, acc_slab, den_slab, max_slab, tmp_ref,
                        band_ref, s_scr, p_scr, d, seq)


def _attn_call(proj):
    _, bsz, seq, _ = proj.shape
    n_slabs = HEADS_PER_GROUP
    n_units = BLOCKS_PER_STEP * HEADS_PER_GROUP
    assert all(GROUP_ORDER[s] == (s + GROUP_ORDER[0]) % N_GROUPS for s in range(N_GROUPS))

    def spec(piece):
        return pl.BlockSpec(
            (None, None, seq, PIECE),
            lambda b, g: (TILE_QKV + (g + GROUP_ORDER[0]) % N_GROUPS, b, 0, piece))

    return pl.pallas_call(
        functools.partial(_attn_kernel, seq=seq),
        name="attn",
        grid=(bsz, N_GROUPS),
        in_specs=[spec(0), spec(1), spec(2)],
        out_specs=pl.BlockSpec((None, seq, GROUP_WIDTH), lambda b, g: (b, 0, 0)),
        out_shape=jax.ShapeDtypeStruct((bsz, seq, GROUP_WIDTH), BF16),
        scratch_shapes=[
            pltpu.VMEM((n_slabs, seq, LANES), F32),
            pltpu.VMEM((n_slabs, seq, LANES), F32),
            pltpu.VMEM((n_slabs, seq, LANES), F32),
            pltpu.VMEM((3 * TMP_UNITS, Q_BLOCK, LANES), F32),
            pltpu.VMEM((N_BANDS, Q_BLOCK, 2 * Q_BLOCK), F32),
            pltpu.VMEM((n_units, Q_BLOCK, 2 * Q_BLOCK), F32),
            pltpu.VMEM((n_units, Q_BLOCK, 2 * Q_BLOCK), BF16),
        ],
        compiler_params=pltpu.CompilerParams(
            dimension_semantics=("arbitrary", "arbitrary"),
            vmem_limit_bytes=VMEM_LIMIT_BYTES),
    )(proj, proj, proj)


def _mix_kernel(zu_ref, zv_ref, ga_ref, gb_ref, yb_ref, x_ref,
                ws_ref, bsp_ref, wa_ref, wb_ref, wo_ref, gain_ref, o_ref, ya_ref):
    group_dim = GMLP_WIDTH // GMLP_GROUPS
    for c in range(0, MIX_ROWS // CHUNK, 2):
        rows0 = slice(c * CHUNK, (c + 1) * CHUNK)
        rows1 = slice((c + 1) * CHUNK, (c + 2) * CHUNK)
        sv = []
        for g in range(GMLP_GROUPS):
            cols = slice(g * group_dim, (g + 1) * group_dim)
            pair = jnp.concatenate([zv_ref[rows0, cols], zv_ref[rows1, cols]], axis=1)
            sv.append(jnp.dot(ws_ref[g], pair, preferred_element_type=F32))
        for k, rows in enumerate((rows0, rows1)):
            sv_k = jnp.concatenate(
                [s[:, k * group_dim:(k + 1) * group_dim] for s in sv], axis=1)
            ya_ref[rows, :] = (zu_ref[rows, :].astype(F32) * (sv_k + bsp_ref[...])).astype(BF16)
    a = jnp.dot(ya_ref[...], wa_ref[...], preferred_element_type=F32)
    b = jnp.dot(yb_ref[...], wb_ref[...], preferred_element_type=F32)
    merged = ga_ref[...].astype(F32) * a + gb_ref[...].astype(F32) * b
    mix = jnp.dot(merged.astype(BF16), wo_ref[...], preferred_element_type=F32)
    ms = jnp.mean(mix * mix, axis=-1, keepdims=True)
    o_ref[...] = x_ref[...] + mix * lax.rsqrt(ms + NORM_EPS) * gain_ref[...]


def _mix_call(proj, y_b, x, w_spatial, b_sp, w_a, w_b, w_out, gain):
    bsz, seq, d_model = x.shape

    def piece(tile, first_piece, n_pieces):
        width = n_pieces * PIECE
        return pl.BlockSpec((None, None, MIX_ROWS, width),
                            lambda b, i: (tile, b, i, first_piece // n_pieces))

    def whole(a):
        return pl.BlockSpec(a.shape, lambda b, i: (0,) * a.ndim)

    return pl.pallas_call(
        _mix_kernel,
        name="mix",
        grid=(bsz, seq // MIX_ROWS),
        in_specs=[
            piece(TILE_A, 2, 1), piece(TILE_B, 2, 1), piece(TILE_A, 0, 2), piece(TILE_B, 0, 2),
            pl.BlockSpec((None, MIX_ROWS, GROUP_WIDTH), lambda b, i: (b, i, 0)),
            pl.BlockSpec((None, MIX_ROWS, d_model), lambda b, i: (b, i, 0)),
            whole(w_spatial), whole(b_sp), whole(w_a), whole(w_b), whole(w_out), whole(gain),
        ],
        out_specs=pl.BlockSpec((None, MIX_ROWS, d_model), lambda b, i: (b, i, 0)),
        out_shape=jax.ShapeDtypeStruct((bsz, seq, d_model), F32),
        scratch_shapes=[pltpu.VMEM((MIX_ROWS, GMLP_WIDTH), BF16)],
        compiler_params=pltpu.CompilerParams(
            dimension_semantics=("parallel", "parallel"),
            vmem_limit_bytes=VMEM_LIMIT_BYTES),
    )(proj, proj, proj, proj, y_b, x, w_spatial, b_sp, w_a, w_b, w_out, gain)


def _mlp_kernel(h_ref, gpre_ref, wu_ref, wd_ref, gpost_ref, o_ref, n_ref, acc_ref, *, d_ff):
    h = h_ref[...]
    ms = jnp.mean(h * h, axis=-1, keepdims=True)
    n_ref[...] = (h * lax.rsqrt(ms + NORM_EPS) * gpre_ref[...]).astype(BF16)
    for c in range(d_ff // MLP_FF_CHUNK):
        cols = slice(c * MLP_FF_CHUNK, (c + 1) * MLP_FF_CHUNK)
        up = jnp.dot(n_ref[...], wu_ref[:, cols], preferred_element_type=F32)
        hid = jnp.square(jnp.maximum(up, 0.0)).astype(BF16)
        part = jnp.dot(hid, wd_ref[cols, :], preferred_element_type=F32)
        if c == 0:
            acc_ref[...] = part
        else:
            acc_ref[...] += part
    out = acc_ref[...]
    ms2 = jnp.mean(out * out, axis=-1, keepdims=True)
    o_ref[...] = h_ref[...] + out * lax.rsqrt(ms2 + NORM_EPS) * gpost_ref[...]


def _mlp_call(h, gain_pre, w_up, w_down, gain_post):
    bsz, seq, d_model = h.shape
    d_ff = w_up.shape[1]
    rows = bsz * seq
    h2 = h.reshape(rows, d_model)

    def whole(a):
        return pl.BlockSpec(a.shape, lambda i: (0,) * a.ndim, pipeline_mode=pl.Buffered(1))

    out = pl.pallas_call(
        functools.partial(_mlp_kernel, d_ff=d_ff),
        name="mlp",
        grid=(rows // MLP_ROWS,),
        in_specs=[
            pl.BlockSpec((MLP_ROWS, d_model), lambda i: (i, 0)),
            whole(gain_pre), whole(w_up), whole(w_down), whole(gain_post),
        ],
        out_specs=pl.BlockSpec((MLP_ROWS, d_model), lambda i: (i, 0)),
        out_shape=jax.ShapeDtypeStruct((rows, d_model), F32),
        scratch_shapes=[
            pltpu.VMEM((MLP_ROWS, d_model), BF16),
            pltpu.VMEM((MLP_ROWS, d_model), F32),
        ],
        compiler_params=pltpu.CompilerParams(
            dimension_semantics=("parallel",),
            vmem_limit_bytes=VMEM_LIMIT_BYTES),
    )(h2, gain_pre, w_up, w_down, gain_post)
    return out.reshape(bsz, seq, d_model)


def _rope_lane_order():
    half = ROPE_DIM // 2
    fill = HEAD_DIM // 2 - half
    return (list(range(half)) + list(range(ROPE_DIM, ROPE_DIM + fill))
            + list(range(half, ROPE_DIM)) + list(range(ROPE_DIM + fill, HEAD_DIM)))


def _source_piece(i):
    n_attn = 3 * N_GROUPS
    first_gate = 2 + n_attn
    in_qkv = i >= TILE_QKV * PIECES_PER_TILE
    a, c = i // PIECES_PER_TILE, i % PIECES_PER_TILE
    head_tiles = jnp.where(c < 2, first_gate + 2 * a + c, a)
    g, section = a - TILE_QKV, c
    return jnp.where(in_qkv, 2 + N_GROUPS * section + g, head_tiles)


def _wprep_kernel(w_ref, sel_ref, o_ref):
    i = pl.program_id(0)
    is_qk = (i >= TILE_QKV * PIECES_PER_TILE) & (i % PIECES_PER_TILE < 2)

    @pl.when(is_qk)
    def _reorder():
        for h in range(HEADS_PER_GROUP):
            hs = slice(h * HEAD_DIM, (h + 1) * HEAD_DIM)
            o_ref[:, hs] = jnp.dot(w_ref[:, hs].astype(BF16), sel_ref[...],
                                   preferred_element_type=F32).astype(BF16)

    @pl.when(jnp.logical_not(is_qk))
    def _copy():
        o_ref[...] = w_ref[...].astype(BF16)


def _regroup_w_in(w_in):
    d_model, d_in = w_in.shape
    select = np.zeros((HEAD_DIM, HEAD_DIM), np.float32)
    select[_rope_lane_order(), np.arange(HEAD_DIM)] = 1.0
    return pl.pallas_call(
        _wprep_kernel,
        name="wprep",
        grid=(d_in // PIECE,),
        in_specs=[
            pl.BlockSpec((d_model, PIECE), lambda i: (0, _source_piece(i))),
            pl.BlockSpec((HEAD_DIM, HEAD_DIM), lambda i: (0, 0)),
        ],
        out_specs=pl.BlockSpec((d_model, PIECE), lambda i: (0, i)),
        out_shape=jax.ShapeDtypeStruct((d_model, d_in), BF16),
        compiler_params=pltpu.CompilerParams(
            dimension_semantics=("parallel",),
            vmem_limit_bytes=VMEM_LIMIT_BYTES),
    )(w_in, jnp.asarray(select, BF16))


def _rope_tables(seq):
    half = ROPE_DIM // 2
    fill = HEAD_DIM // 2 - half
    inv_freq = np.float32(ROPE_THETA) ** (-np.arange(0, ROPE_DIM, 2, dtype=np.float32) / ROPE_DIM)
    ang = np.arange(seq, dtype=np.float32)[:, None] * inv_freq[None, :].astype(np.float32)
    cos, sin = np.cos(ang), np.sin(ang)
    ones = np.ones((seq, fill), np.float32)
    cos_l = np.concatenate([cos, ones, cos, ones], axis=1)
    sin_l = np.concatenate([-sin, 0.0 * ones, sin, 0.0 * ones], axis=1)
    nat = np.stack([cos_l, sin_l])
    per_group = []
    for _, d in ATTN_PATTERNS:
        by_residue = nat.reshape(2, seq // d, d, HEAD_DIM).transpose(0, 2, 1, 3)
        residues = [_slot_residue(slot, d) for slot in range(d)]
        per_group.append(by_residue[:, residues].reshape(2, seq, HEAD_DIM))
    return jnp.asarray(np.stack(per_group), F32)


def _layer(h, norm_mix_pre, w_in, b_gate, ln_v_gain, ln_v_bias, w_spatial, b_spatial,
           w_branch_a, w_branch_b, w_out, norm_mix_post, norm_mlp_pre, w_up, w_down,
           norm_mlp_post):
    seq = h.shape[1]
    row = lambda v: v.reshape(1, -1).astype(F32)
    proj = _proj_call(h, row(norm_mix_pre), _regroup_w_in(w_in), row(b_gate), row(ln_v_gain),
                      row(ln_v_bias), _rope_tables(seq))
    y_b = _attn_call(proj)
    b_sp = jnp.repeat(b_spatial.T.astype(F32), GMLP_WIDTH // GMLP_GROUPS, axis=1)
    h = _mix_call(proj, y_b, h, w_spatial.astype(BF16), b_sp, w_branch_a.astype(BF16),
                  w_branch_b.astype(BF16), w_out.astype(BF16), row(norm_mix_post))
    return _mlp_call(h, row(norm_mlp_pre), w_up.astype(BF16), w_down.astype(BF16),
                     row(norm_mlp_post))


def kernel(x, norm_mix_pre, w_in, b_gate, ln_v_gain, ln_v_bias, w_spatial, b_spatial,
           w_branch_a, w_branch_b, w_out, norm_mix_post, norm_mlp_pre, w_up, w_down,
           norm_mlp_post):
    h = x
    for l in range(w_in.shape[0]):
        h = _layer(h, norm_mix_pre[l], w_in[l], b_gate[l], ln_v_gain[l], ln_v_bias[l],
                   w_spatial[l], b_spatial[l], w_branch_a[l], w_branch_b[l], w_out[l],
                   norm_mix_post[l], norm_mlp_pre[l], w_up[l], w_down[l], norm_mlp_post[l])
    return h
```

```python
import functools
import math

import jax
import jax.numpy as jnp
import numpy as np
from jax import lax
from jax.experimental import pallas as pl
from jax.experimental.pallas import tpu as pltpu

GMLP_WIDTH = 512
GMLP_GROUPS = 4
CHUNK = 128
ATTN_PATTERNS = ((128, 1), (512, 4), (2048, 16))
N_GROUPS = len(ATTN_PATTERNS)
HEADS_PER_GROUP = 4
HEAD_DIM = 128
GROUP_WIDTH = HEADS_PER_GROUP * HEAD_DIM
ROPE_DIM = HEAD_DIM // 4
ROPE_THETA = 500000.0
NORM_EPS = 1e-6
MASK_VALUE = -1e30
LOG2E = math.log2(math.e)
QK_SCALE = LOG2E / math.sqrt(HEAD_DIM)
RADIUS = 64
assert all(w // (2 * d) == RADIUS for w, d in ATTN_PATTERNS)
DEINT_STRIDE = 4
assert tuple(d for _, d in ATTN_PATTERNS) == (1, DEINT_STRIDE, DEINT_STRIDE ** 2)


def _slot_residue(slot, d):
    if d == DEINT_STRIDE ** 2:
        return DEINT_STRIDE * (slot % DEINT_STRIDE) + slot // DEINT_STRIDE
    return slot


LANES = 128
VMEM_LIMIT_BYTES = 56 * 1024 * 1024

PIECE = 512
PIECES_PER_TILE = 3
TN = PIECES_PER_TILE * PIECE
PROLOGUE_ROWS = 256
PROJ_ROWS = 256
PROJ_TALL_ROWS = 1024
Q_BLOCK = 128
BLOCKS_PER_STEP = 16
TMP_UNITS = 16
N_BANDS = 3
GROUP_ORDER = (1, 2, 0)
MIX_ROWS = 1024
MLP_ROWS = 1024
MLP_FF_CHUNK = 1024

BF16 = jnp.bfloat16
F32 = jnp.float32

TILE_A, TILE_B, TILE_QKV = 0, 1, 2
N_TILES = TILE_QKV + N_GROUPS


def _gelu_tanh(x):
    c = math.sqrt(2.0 / math.pi)
    return x * (0.5 * (1.0 + jnp.tanh(c * (x + 0.044715 * (x * x * x)))))


def _sigmoid(x):
    return 1.0 / (1.0 + jnp.exp(-x))


def _proj_kernel(x_ref, gain_ref, w_ref, bg_ref, lng_ref, lnb_ref, rope_ref, o_ref,
                 lhs_ref, slab_ref, *, seq, d_model):
    j = pl.program_id(1)
    n_slabs = d_model // LANES

    @pl.when(j == 0)
    def _prologue():
        st = DEINT_STRIDE
        cnt1 = PROLOGUE_ROWS // st
        cnt2 = cnt1 // st

        def body(c, carry):
            r0 = pl.multiple_of(c * PROLOGUE_ROWS, PROLOGUE_ROWS)
            xc = x_ref[pl.ds(r0, PROLOGUE_ROWS), :]
            ms = jnp.mean(xc * xc, axis=-1, keepdims=True)
            n = xc * lax.rsqrt(ms + NORM_EPS) * gain_ref[...]
            lhs_ref[0, pl.ds(r0, PROLOGUE_ROWS), :] = n.astype(BF16)
            for s in range(n_slabs):
                lanes = slice(s * LANES, (s + 1) * LANES)
                slab_ref[0, s] = n[:, lanes]
                for r1 in range(st):
                    piece = slab_ref[0, s, pl.ds(r1, cnt1, stride=st), :]
                    dst1 = pl.multiple_of(r1 * (seq // st) + c * cnt1, cnt1)
                    lhs_ref[1, pl.ds(dst1, cnt1), lanes] = piece.astype(BF16)
                    slab_ref[1, s, r1 * cnt1:(r1 + 1) * cnt1, :] = piece
                for r1 in range(st):
                    for r2 in range(st):
                        sub = slab_ref[1, s, pl.ds(r1 * cnt1 + r2, cnt2, stride=st), :]
                        slot = r1 * st + r2
                        dst2 = pl.multiple_of(slot * (seq // (st * st)) + c * cnt2, cnt2)
                        lhs_ref[2, pl.ds(dst2, cnt2), lanes] = sub.astype(BF16)
            return carry

        lax.fori_loop(0, seq // PROLOGUE_ROWS, body, 0)

    def matmul_rows(lhs_idx, epilogues):
        for c in range(seq // PROJ_ROWS):
            r0 = c * PROJ_ROWS
            acc = jnp.dot(lhs_ref[lhs_idx, pl.ds(r0, PROJ_ROWS), :], w_ref[...],
                          preferred_element_type=F32)
            for c0, width, fn in epilogues:
                o_ref[pl.ds(r0, PROJ_ROWS), c0:c0 + width] = (
                    fn(acc[:, c0:c0 + width], r0, PROJ_ROWS).astype(BF16))

    def matmul_slices(lhs_idx, epilogues):
        for c in range(seq // PROJ_TALL_ROWS):
            r0 = c * PROJ_TALL_ROWS
            for c0, width, fn in epilogues:
                acc = jnp.dot(lhs_ref[lhs_idx, pl.ds(r0, PROJ_TALL_ROWS), :],
                              w_ref[:, c0:c0 + width], preferred_element_type=F32)
                o_ref[pl.ds(r0, PROJ_TALL_ROWS), c0:c0 + width] = (
                    fn(acc, r0, PROJ_TALL_ROWS).astype(BF16))

    def gates(acc, r0, rows):
        return _sigmoid(acc + bg_ref[...])

    def layer_norm_gelu(acc, r0, rows):
        z = _gelu_tanh(acc)
        mu = jnp.mean(z, axis=-1, keepdims=True)
        zc = z - mu
        var = jnp.mean(zc * zc, axis=-1, keepdims=True)
        return zc * lax.rsqrt(var + NORM_EPS) * lng_ref[...] + lnb_ref[...]

    def rotary(scale):
        def fn(acc, r0, rows):
            cos = rope_ref[0, pl.ds(r0, rows), :]
            sin = rope_ref[1, pl.ds(r0, rows), :]
            heads = []
            for h in range(HEADS_PER_GROUP):
                t = acc[:, h * HEAD_DIM:(h + 1) * HEAD_DIM]
                out = t * cos + pltpu.roll(t, HEAD_DIM // 2, 1) * sin
                heads.append(out if scale is None else out * scale)
            return jnp.concatenate(heads, axis=1)

        return fn

    @pl.when(j == TILE_A)
    def _tile_a():
        matmul_rows(0, [(0, 2 * PIECE, gates),
                        (2 * PIECE, PIECE, lambda acc, r0, rows: _gelu_tanh(acc))])

    @pl.when(j == TILE_B)
    def _tile_b():
        matmul_rows(0, [(0, 2 * PIECE, gates), (2 * PIECE, PIECE, layer_norm_gelu)])

    @pl.when(j >= TILE_QKV)
    def _tile_qkv():
        matmul_slices(j - TILE_QKV, [(0, PIECE, rotary(QK_SCALE)),
                                     (PIECE, PIECE, rotary(None)),
                                     (2 * PIECE, PIECE, lambda acc, r0, rows: acc)])


def _proj_call(x, gain, w_in, b_gate, ln_gain, ln_bias, rope_tab):
    bsz, seq, d_model = x.shape
    assert w_in.shape[1] == N_TILES * TN
    kern = functools.partial(_proj_kernel, seq=seq, d_model=d_model)
    return pl.pallas_call(
        kern,
        name="proj",
        grid=(bsz, N_TILES),
        in_specs=[
            pl.BlockSpec((None, seq, d_model), lambda b, j: (b, 0, 0)),
            pl.BlockSpec((1, d_model), lambda b, j: (0, 0)),
            pl.BlockSpec((d_model, TN), lambda b, j: (0, j)),
            pl.BlockSpec((1, 2 * PIECE), lambda b, j: (0, jnp.minimum(j, TILE_B))),
            pl.BlockSpec((1, PIECE), lambda b, j: (0, 0)),
            pl.BlockSpec((1, PIECE), lambda b, j: (0, 0)),
            pl.BlockSpec((None, 2, seq, HEAD_DIM),
                         lambda b, j: (jnp.maximum(j - TILE_QKV, 0), 0, 0, 0)),
        ],
        out_specs=pl.BlockSpec((None, None, seq, TN), lambda b, j: (j, b, 0, 0)),
        out_shape=jax.ShapeDtypeStruct((N_TILES, bsz, seq, TN), BF16),
        scratch_shapes=[
            pltpu.VMEM((N_GROUPS, seq, d_model), BF16),
            pltpu.VMEM((2, d_model // LANES, PROLOGUE_ROWS, LANES), F32),
        ],
        compiler_params=pltpu.CompilerParams(
            dimension_semantics=("arbitrary", "arbitrary"),
            vmem_limit_bytes=VMEM_LIMIT_BYTES),
    )(x, gain, w_in, b_gate, ln_gain, ln_bias, rope_tab)


def _merge(old, new):
    acc0, den0, max0 = old
    acc1, den1, max1 = new
    m = jnp.maximum(max0, max1)
    w0 = jnp.exp2(max0 - m)
    w1 = jnp.exp2(max1 - m)
    return acc0 * w0 + acc1 * w1, den0 * w0 + den1 * w1, m


def _attn_group(q_ref, k_ref, v_ref, y_ref, acc_slab, den_slab, max_slab, tmp_ref,
                band_ref, s_scr, p_scr, d, seq):
    st = DEINT_STRIDE
    sub_len = seq // d
    n_blk = sub_len // Q_BLOCK
    win = min(2 * Q_BLOCK, sub_len)
    ones = jnp.ones((win, HEAD_DIM), BF16)
    slabs = (acc_slab, den_slab, max_slab)
    bps = BLOCKS_PER_STEP

    def body(step, carry):
        blocks = []
        for e in range(bps):
            f = step * bps + e
            slot = f // n_blk
            q_off = (f % n_blk) * Q_BLOCK
            k_off = jnp.clip(q_off - RADIUS, 0, sub_len - win)
            blocks.append(dict(
                slot=slot, q_off=q_off, band=(q_off - k_off) // RADIUS,
                q0=pl.multiple_of(slot * sub_len + q_off, Q_BLOCK),
                k0=pl.multiple_of(slot * sub_len + k_off, RADIUS)))
        units = [(e, h) for e in range(bps) for h in range(HEADS_PER_GROUP)]

        for u, (e, h) in enumerate(units):
            blk = blocks[e]
            hs = slice(h * HEAD_DIM, (h + 1) * HEAD_DIM)
            s_scr[u, :, :win] = lax.dot_general(
                q_ref[pl.ds(blk["q0"], Q_BLOCK), hs], k_ref[pl.ds(blk["k0"], win), hs],
                (((1,), (1,)), ((), ())), preferred_element_type=F32)

        row_max = []
        for u, (e, h) in enumerate(units):
            s = jnp.where(band_ref[blocks[e]["band"], :, :win] > 0.0, s_scr[u, :, :win],
                          MASK_VALUE)
            m = jnp.max(s, axis=-1, keepdims=True)
            p_scr[u, :, :win] = jnp.exp2(s - m).astype(BF16)
            row_max.append(m)

        for u, (e, h) in enumerate(units):
            blk = blocks[e]
            hs = slice(h * HEAD_DIM, (h + 1) * HEAD_DIM)
            v_ext = jnp.concatenate([v_ref[pl.ds(blk["k0"], win), hs], ones], axis=1)
            oe = jnp.dot(p_scr[u, :, :win], v_ext, preferred_element_type=F32)
            new = (oe[:, :HEAD_DIM], oe[:, HEAD_DIM:],
                   jnp.broadcast_to(row_max[u], (Q_BLOCK, HEAD_DIM)))
            if d == st:
                rows = pl.ds(blk["q0"], Q_BLOCK)
                for slab, val in zip(slabs, new):
                    slab[h, rows, :] = val
            elif d == st * st:
                r1, r2 = blk["slot"] // st, blk["slot"] % st
                rows = pl.ds(r1 * (seq // st) + blk["q_off"] * st + r2, Q_BLOCK, stride=st)
                merged = _merge([slab[h, rows, :] for slab in slabs], new)
                for slab, val in zip(slabs, merged):
                    slab[h, rows, :] = val
            else:
                cnt = Q_BLOCK // st
                old = []
                for k, slab in enumerate(slabs):
                    t = len(slabs) * (u % TMP_UNITS) + k
                    for r in range(st):
                        src = pl.multiple_of(r * (seq // st) + blk["q0"] // st, cnt)
                        tmp_ref[t, pl.ds(r, cnt, stride=st), :] = slab[h, pl.ds(src, cnt), :]
                    old.append(tmp_ref[t])
                acc, den, _ = _merge(old, new)
                y_ref[pl.ds(blk["q0"], Q_BLOCK), hs] = (acc / den).astype(BF16)
        return carry

    lax.fori_loop(0, d * n_blk // bps, body, 0)


def _attn_kernel(q_ref, k_ref, v_ref, y_ref, acc_slab, den_slab, max_slab, tmp_ref,
                 band_ref, s_scr, p_scr, *, seq):
    g = pl.program_id(1)

    @pl.when((pl.program_id(0) == 0) & (g == 0))
    def _band_masks():
        row = lax.broadcasted_iota(jnp.int32, (Q_BLOCK, 2 * Q_BLOCK), 0)
        col = lax.broadcasted_iota(jnp.int32, (Q_BLOCK, 2 * Q_BLOCK), 1)
        for band in range(N_BANDS):
            valid = jnp.abs(row - col + band * RADIUS) <= RADIUS
            band_ref[band] = jnp.where(valid, 1.0, 0.0).astype(F32)

    for step, gi in enumerate(GROUP_ORDER):
        @pl.when(g == step)
        def _group(d=ATTN_PATTERNS[gi][1]):
            _attn_group(q_ref, k_ref, v_ref, y_ref, acc_slab, den_slab, max_slab, tmp_ref,
                        band_ref, s_scr, p_scr, d, seq)


def _attn_call(proj):
    _, bsz, seq, _ = proj.shape
    n_slabs = HEADS_PER_GROUP
    n_units = BLOCKS_PER_STEP * HEADS_PER_GROUP
    assert all(GROUP_ORDER[s] == (s + GROUP_ORDER[0]) % N_GROUPS for s in range(N_GROUPS))

    def spec(piece):
        return pl.BlockSpec(
            (None, None, seq, PIECE),
            lambda b, g: (TILE_QKV + (g + GROUP_ORDER[0]) % N_GROUPS, b, 0, piece))

    return pl.pallas_call(
        functools.partial(_attn_kernel, seq=seq),
        name="attn",
        grid=(bsz, N_GROUPS),
        in_specs=[spec(0), spec(1), spec(2)],
        out_specs=pl.BlockSpec((None, seq, GROUP_WIDTH), lambda b, g: (b, 0, 0)),
        out_shape=jax.ShapeDtypeStruct((bsz, seq, GROUP_WIDTH), BF16),
        scratch_shapes=[
            pltpu.VMEM((n_slabs, seq, LANES), F32),
            pltpu.VMEM((n_slabs, seq, LANES), F32),
            pltpu.VMEM((n_slabs, seq, LANES), F32),
            pltpu.VMEM((3 * TMP_UNITS, Q_BLOCK, LANES), F32),
            pltpu.VMEM((N_BANDS, Q_BLOCK, 2 * Q_BLOCK), F32),
            pltpu.VMEM((n_units, Q_BLOCK, 2 * Q_BLOCK), F32),
            pltpu.VMEM((n_units, Q_BLOCK, 2 * Q_BLOCK), BF16),
        ],
        compiler_params=pltpu.CompilerParams(
            dimension_semantics=("arbitrary", "arbitrary"),
            vmem_limit_bytes=VMEM_LIMIT_BYTES),
    )(proj, proj, proj)


def _mix_kernel(zu_ref, zv_ref, ga_ref, gb_ref, yb_ref, x_ref,
                ws_ref, bsp_ref, wa_ref, wb_ref, wo_ref, gain_ref, o_ref, ya_ref):
    group_dim = GMLP_WIDTH // GMLP_GROUPS
    for c in range(0, MIX_ROWS // CHUNK, 2):
        rows0 = slice(c * CHUNK, (c + 1) * CHUNK)
        rows1 = slice((c + 1) * CHUNK, (c + 2) * CHUNK)
        sv = []
        for g in range(GMLP_GROUPS):
            cols = slice(g * group_dim, (g + 1) * group_dim)
            pair = jnp.concatenate([zv_ref[rows0, cols], zv_ref[rows1, cols]], axis=1)
            sv.append(jnp.dot(ws_ref[g], pair, preferred_element_type=F32))
        for k, rows in enumerate((rows0, rows1)):
            sv_k = jnp.concatenate(
                [s[:, k * group_dim:(k + 1) * group_dim] for s in sv], axis=1)
            ya_ref[rows, :] = (zu_ref[rows, :].astype(F32) * (sv_k + bsp_ref[...])).astype(BF16)
    a = jnp.dot(ya_ref[...], wa_ref[...], preferred_element_type=F32)
    b = jnp.dot(yb_ref[...], wb_ref[...], preferred_element_type=F32)
    merged = ga_ref[...].astype(F32) * a + gb_ref[...].astype(F32) * b
    mix = jnp.dot(merged.astype(BF16), wo_ref[...], preferred_element_type=F32)
    ms = jnp.mean(mix * mix, axis=-1, keepdims=True)
    o_ref[...] = x_ref[...] + mix * lax.rsqrt(ms + NORM_EPS) * gain_ref[...]


def _mix_call(proj, y_b, x, w_spatial, b_sp, w_a, w_b, w_out, gain):
    bsz, seq, d_model = x.shape

    def piece(tile, first_piece, n_pieces):
        width = n_pieces * PIECE
        return pl.BlockSpec((None, None, MIX_ROWS, width),
                            lambda b, i: (tile, b, i, first_piece // n_pieces))

    def whole(a):
        return pl.BlockSpec(a.shape, lambda b, i: (0,) * a.ndim)

    return pl.pallas_call(
        _mix_kernel,
        name="mix",
        grid=(bsz, seq // MIX_ROWS),
        in_specs=[
            piece(TILE_A, 2, 1), piece(TILE_B, 2, 1), piece(TILE_A, 0, 2), piece(TILE_B, 0, 2),
            pl.BlockSpec((None, MIX_ROWS, GROUP_WIDTH), lambda b, i: (b, i, 0)),
            pl.BlockSpec((None, MIX_ROWS, d_model), lambda b, i: (b, i, 0)),
            whole(w_spatial), whole(b_sp), whole(w_a), whole(w_b), whole(w_out), whole(gain),
        ],
        out_specs=pl.BlockSpec((None, MIX_ROWS, d_model), lambda b, i: (b, i, 0)),
        out_shape=jax.ShapeDtypeStruct((bsz, seq, d_model), F32),
        scratch_shapes=[pltpu.VMEM((MIX_ROWS, GMLP_WIDTH), BF16)],
        compiler_params=pltpu.CompilerParams(
            dimension_semantics=("parallel", "parallel"),
            vmem_limit_bytes=VMEM_LIMIT_BYTES),
    )(proj, proj, proj, proj, y_b, x, w_spatial, b_sp, w_a, w_b, w_out, gain)


def _mlp_kernel(h_ref, gpre_ref, wu_ref, wd_ref, gpost_ref, o_ref, n_ref, acc_ref, *, d_ff):
    h = h_ref[...]
    ms = jnp.mean(h * h, axis=-1, keepdims=True)
    n_ref[...] = (h * lax.rsqrt(ms + NORM_EPS) * gpre_ref[...]).astype(BF16)
    for c in range(d_ff // MLP_FF_CHUNK):
        cols = slice(c * MLP_FF_CHUNK, (c + 1) * MLP_FF_CHUNK)
        up = jnp.dot(n_ref[...], wu_ref[:, cols], preferred_element_type=F32)
        hid = jnp.square(jnp.maximum(up, 0.0)).astype(BF16)
        part = jnp.dot(hid, wd_ref[cols, :], preferred_element_type=F32)
        if c == 0:
            acc_ref[...] = part
        else:
            acc_ref[...] += part
    out = acc_ref[...]
    ms2 = jnp.mean(out * out, axis=-1, keepdims=True)
    o_ref[...] = h_ref[...] + out * lax.rsqrt(ms2 + NORM_EPS) * gpost_ref[...]


def _mlp_call(h, gain_pre, w_up, w_down, gain_post):
    bsz, seq, d_model = h.shape
    d_ff = w_up.shape[1]
    rows = bsz * seq
    h2 = h.reshape(rows, d_model)

    def whole(a):
        return pl.BlockSpec(a.shape, lambda i: (0,) * a.ndim, pipeline_mode=pl.Buffered(1))

    out = pl.pallas_call(
        functools.partial(_mlp_kernel, d_ff=d_ff),
        name="mlp",
        grid=(rows // MLP_ROWS,),
        in_specs=[
            pl.BlockSpec((MLP_ROWS, d_model), lambda i: (i, 0)),
            whole(gain_pre), whole(w_up), whole(w_down), whole(gain_post),
        ],
        out_specs=pl.BlockSpec((MLP_ROWS, d_model), lambda i: (i, 0)),
        out_shape=jax.ShapeDtypeStruct((rows, d_model), F32),
        scratch_shapes=[
            pltpu.VMEM((MLP_ROWS, d_model), BF16),
            pltpu.VMEM((MLP_ROWS, d_model), F32),
        ],
        compiler_params=pltpu.CompilerParams(
            dimension_semantics=("parallel",),
            vmem_limit_bytes=VMEM_LIMIT_BYTES),
    )(h2, gain_pre, w_up, w_down, gain_post)
    return out.reshape(bsz, seq, d_model)


def _rope_lane_order():
    half = ROPE_DIM // 2
    fill = HEAD_DIM // 2 - half
    return (list(range(half)) + list(range(ROPE_DIM, ROPE_DIM + fill))
            + list(range(half, ROPE_DIM)) + list(range(ROPE_DIM + fill, HEAD_DIM)))


def _source_piece(i):
    n_attn = 3 * N_GROUPS
    first_gate = 2 + n_attn
    in_qkv = i >= TILE_QKV * PIECES_PER_TILE
    a, c = i // PIECES_PER_TILE, i % PIECES_PER_TILE
    head_tiles = jnp.where(c < 2, first_gate + 2 * a + c, a)
    g, section = a - TILE_QKV, c
    return jnp.where(in_qkv, 2 + N_GROUPS * section + g, head_tiles)


def _wprep_kernel(*refs):
    piece_refs, sel_ref, o_ref = refs[:PIECES_PER_TILE], refs[PIECES_PER_TILE], refs[-1]
    is_qkv_tile = pl.program_id(0) >= TILE_QKV
    for c, w_ref in enumerate(piece_refs):
        cols = slice(c * PIECE, (c + 1) * PIECE)

        def copy(w_ref=w_ref, cols=cols):
            o_ref[:, cols] = w_ref[...].astype(BF16)

        def reorder(w_ref=w_ref, c=c):
            for h in range(HEADS_PER_GROUP):
                hs = slice(h * HEAD_DIM, (h + 1) * HEAD_DIM)
                o_ref[:, c * PIECE + h * HEAD_DIM:c * PIECE + (h + 1) * HEAD_DIM] = jnp.dot(
                    w_ref[:, hs].astype(BF16), sel_ref[...],
                    preferred_element_type=F32).astype(BF16)

        if c < 2:
            pl.when(is_qkv_tile)(reorder)
            pl.when(jnp.logical_not(is_qkv_tile))(copy)
        else:
            copy()


def _regroup_w_in(w_in):
    d_model, d_in = w_in.shape
    select = np.zeros((HEAD_DIM, HEAD_DIM), np.float32)
    select[_rope_lane_order(), np.arange(HEAD_DIM)] = 1.0
    return pl.pallas_call(
        _wprep_kernel,
        name="wprep",
        grid=(d_in // TN,),
        in_specs=[
            pl.BlockSpec((d_model, PIECE),
                         lambda t, c=c: (0, _source_piece(t * PIECES_PER_TILE + c)))
            for c in range(PIECES_PER_TILE)
        ] + [pl.BlockSpec((HEAD_DIM, HEAD_DIM), lambda t: (0, 0))],
        out_specs=pl.BlockSpec((d_model, TN), lambda t: (0, t)),
        out_shape=jax.ShapeDtypeStruct((d_model, d_in), BF16),
        compiler_params=pltpu.CompilerParams(
            dimension_semantics=("parallel",),
            vmem_limit_bytes=VMEM_LIMIT_BYTES),
    )(*([w_in] * PIECES_PER_TILE), jnp.asarray(select, BF16))


def _rope_tables(seq):
    half = ROPE_DIM // 2
    fill = HEAD_DIM // 2 - half
    inv_freq = np.float32(ROPE_THETA) ** (-np.arange(0, ROPE_DIM, 2, dtype=np.float32) / ROPE_DIM)
    ang = np.arange(seq, dtype=np.float32)[:, None] * inv_freq[None, :].astype(np.float32)
    cos, sin = np.cos(ang), np.sin(ang)
    ones = np.ones((seq, fill), np.float32)
    cos_l = np.concatenate([cos, ones, cos, ones], axis=1)
    sin_l = np.concatenate([-sin, 0.0 * ones, sin, 0.0 * ones], axis=1)
    nat = np.stack([cos_l, sin_l])
    per_group = []
    for _, d in ATTN_PATTERNS:
        by_residue = nat.reshape(2, seq // d, d, HEAD_DIM).transpose(0, 2, 1, 3)
        residues = [_slot_residue(slot, d) for slot in range(d)]
        per_group.append(by_residue[:, residues].reshape(2, seq, HEAD_DIM))
    return jnp.asarray(np.stack(per_group), F32)


def _layer(h, norm_mix_pre, w_in, b_gate, ln_v_gain, ln_v_bias, w_spatial, b_spatial,
           w_branch_a, w_branch_b, w_out, norm_mix_post, norm_mlp_pre, w_up, w_down,
           norm_mlp_post):
    seq = h.shape[1]
    row = lambda v: v.reshape(1, -1).astype(F32)
    proj = _proj_call(h, row(norm_mix_pre), _regroup_w_in(w_in), row(b_gate), row(ln_v_gain),
                      row(ln_v_bias), _rope_tables(seq))
    y_b = _attn_call(proj)
    b_sp = jnp.repeat(b_spatial.T.astype(F32), GMLP_WIDTH // GMLP_GROUPS, axis=1)
    h = _mix_call(proj, y_b, h, w_spatial.astype(BF16), b_sp, w_branch_a.astype(BF16),
                  w_branch_b.astype(BF16), w_out.astype(BF16), row(norm_mix_post))
    return _mlp_call(h, row(norm_mlp_pre), w_up.astype(BF16), w_down.astype(BF16),
                     row(norm_mlp_post))


def kernel(x, norm_mix_pre, w_in, b_gate, ln_v_gain, ln_v_bias, w_spatial, b_spatial,
           w_branch_a, w_branch_b, w_out, norm_mix_post, norm_mlp_pre, w_up, w_down,
           norm_mlp_post):
    h = x
    for l in range(w_in.shape[0]):
        h = _layer(h, norm_mix_pre[l], w_in[l], b_gate[l], ln_v_gain[l], ln_v_bias[l],
                   w_spatial[l], b_spatial[l], w_branch_a[l], w_branch_b[l], w_out[l],
                   norm_mix_post[l], norm_mlp_pre[l], w_up[l], w_down[l], norm_mlp_post[l])
    return h
```
